```python
import jax, jax.numpy as jnp
from jax import lax
import numpy as np

D_MODEL = 1024
BATCH = 8
SEQ = 4096
DEPTH = 4

N_MIXERS = 2
HEAD_DIM = 64
N_HEADS = D_MODEL // HEAD_DIM
MIX_WIDTH = N_HEADS * HEAD_DIM
N_KV_A = max(1, N_HEADS // 8)
N_KV_B = max(1, N_HEADS // 4)
WINDOW = 128
MOBA_BLOCK = 256
MOBA_TOPK = 3
MOBA_Q_CHUNK = 16
ROPE_THETA = 10000.0
NORM_EPS = 1e-5
NEG = -1e30
N_LAYERS_A = (DEPTH + 1) // 2
N_LAYERS_B = DEPTH // 2
IN_A = 2 * MIX_WIDTH + 2 * N_KV_A * HEAD_DIM
IN_B = 2 * MIX_WIDTH + 2 * N_KV_B * HEAD_DIM

kernel_name = "hybrid_swa_sink_moba_gated"


def rmsnorm(x, g):
    xf = x.astype(jnp.float32)
    y = xf * lax.rsqrt(jnp.mean(xf * xf, axis=-1, keepdims=True) + NORM_EPS)
    return (y * g.astype(jnp.float32)).astype(x.dtype)


def rope_tables(seq):
    pos = jnp.arange(seq, dtype=jnp.float32)
    inv = ROPE_THETA ** (-jnp.arange(0, HEAD_DIM, 2, dtype=jnp.float32) / HEAD_DIM)
    ang = pos[:, None] * inv[None, :]
    return jnp.cos(ang), jnp.sin(ang)


def apply_rope(x, cos, sin):
    half = HEAD_DIM // 2
    x1, x2 = x[..., :half], x[..., half:]
    c, s = cos[None, :, None, :], sin[None, :, None, :]
    out = jnp.concatenate([x1 * c - x2 * s, x2 * c + x1 * s], axis=-1)
    return out.astype(x.dtype)


def split_proj(h, w_in, n_kv):
    B, S, _ = h.shape
    proj = h @ w_in
    kvw = n_kv * HEAD_DIM
    q = proj[..., :MIX_WIDTH].reshape(B, S, N_HEADS, HEAD_DIM)
    k = proj[..., MIX_WIDTH:MIX_WIDTH + kvw].reshape(B, S, n_kv, HEAD_DIM)
    v = proj[..., MIX_WIDTH + kvw:MIX_WIDTH + 2 * kvw].reshape(B, S, n_kv, HEAD_DIM)
    z = proj[..., MIX_WIDTH + 2 * kvw:]
    return q, k, v, z


def sliding_window_sink_attention(q, k, v, sinks):
    B, S, H, d = q.shape
    kvh = k.shape[2]
    G = H // kvh
    nq = S // WINDOW
    scale = 1.0 / np.sqrt(d)
    qb = q.reshape(B, nq, WINDOW, kvh, G, d)
    pad = ((0, 0), (WINDOW, 0), (0, 0), (0, 0))
    kb = jnp.pad(k, pad).reshape(B, nq + 1, WINDOW, kvh, d)
    vb = jnp.pad(v, pad).reshape(B, nq + 1, WINDOW, kvh, d)
    kband = jnp.concatenate([kb[:, :-1], kb[:, 1:]], axis=2)
    vband = jnp.concatenate([vb[:, :-1], vb[:, 1:]], axis=2)
    s = jnp.einsum('bnikgd,bnjkd->bnkgij', qb, kband).astype(jnp.float32) * scale
    i = jnp.arange(WINDOW)[:, None]
    j = jnp.arange(2 * WINDOW)[None, :]
    band = (j > i) & (j <= i + WINDOW)
    not_pad = (jnp.arange(nq)[:, None, None] > 0) | (j[None] >= WINDOW)
    mask = band[None] & not_pad
    s = jnp.where(mask[None, :, None, None], s, NEG)
    sink = jnp.broadcast_to(sinks.astype(jnp.float32).reshape(kvh, G)[None, None, :, :, None, None],
                            s.shape[:-1] + (1,))
    p = jax.nn.softmax(jnp.concatenate([s, sink], axis=-1), axis=-1)[..., :-1]
    o = jnp.einsum('bnkgij,bnjkd->bnikgd', p.astype(v.dtype), vband)
    return o.reshape(B, S, H, d)


def moba_attention(q, k, v):
    B, S, H, d = q.shape
    kvh = k.shape[2]
    G = H // kvh
    nb = -(-S // MOBA_BLOCK)
    Sp = nb * MOBA_BLOCK
    scale = 1.0 / np.sqrt(d)
    pad = ((0, 0), (0, Sp - S), (0, 0), (0, 0))
    kblk = jnp.pad(k, pad).reshape(B, nb, MOBA_BLOCK, kvh, d).transpose(0, 3, 1, 2, 4)
    vblk = jnp.pad(v, pad).reshape(B, nb, MOBA_BLOCK, kvh, d).transpose(0, 3, 1, 2, 4)
    kmean = jnp.mean(kblk.astype(jnp.float32), axis=3)
    qg = q.reshape(B, S, kvh, G, d)
    gate = jnp.einsum('bskgd,bknd->bskgn', qg.astype(jnp.float32), kmean)
    qblock = jnp.arange(S) // MOBA_BLOCK
    past = jnp.arange(nb)[None, :] < qblock[:, None]
    gate = jnp.where(past[None, :, None, None, :], gate, NEG)
    kk = min(MOBA_TOPK, nb)
    _, idx = lax.top_k(gate, kk)
    nc = S // MOBA_Q_CHUNK
    q_c = qg.reshape(B, nc, MOBA_Q_CHUNK, kvh, G, d).transpose(1, 0, 2, 3, 4, 5)
    idx_c = idx.reshape(B, nc, MOBA_Q_CHUNK, kvh, G, kk).transpose(1, 0, 2, 3, 4, 5)
    bi = jnp.arange(B)[:, None, None, None, None]
    ki = jnp.arange(kvh)[None, None, :, None, None]

    def chunk(args):
        ci, qc, ic = args
        t0 = ci * MOBA_Q_CHUNK
        t = t0 + jnp.arange(MOBA_Q_CHUNK)
        cblk = t0 // MOBA_BLOCK
        ksel = kblk[bi, ki, ic]
        vsel = vblk[bi, ki, ic]
        s_sel = jnp.einsum('bqkgd,bqkgsnd->bqkgsn', qc, ksel).astype(jnp.float32) * scale
        slot_ok = jnp.arange(kk) < cblk
        s_sel = jnp.where(slot_ok[:, None], s_sel, NEG)
        kown = lax.dynamic_index_in_dim(kblk, cblk, axis=2, keepdims=False)
        vown = lax.dynamic_index_in_dim(vblk, cblk, axis=2, keepdims=False)
        s_own = jnp.einsum('bqkgd,bknd->bqkgn', qc, kown).astype(jnp.float32) * scale
        kpos = cblk * MOBA_BLOCK + jnp.arange(MOBA_BLOCK)
        causal = kpos[None, :] <= t[:, None]
        s_own = jnp.where(causal[None, :, None, None, :], s_own, NEG)
        nsel = kk * MOBA_BLOCK
        logits = jnp.concatenate([s_sel.reshape(B, MOBA_Q_CHUNK, kvh, G, nsel), s_own], axis=-1)
        p = jax.nn.softmax(logits, axis=-1).astype(v.dtype)
        o = jnp.einsum('bqkgm,bqkgmd->bqkgd', p[..., :nsel],
                       vsel.reshape(B, MOBA_Q_CHUNK, kvh, G, nsel, d))
        o = o + jnp.einsum('bqkgn,bknd->bqkgd', p[..., nsel:], vown)
        return o

    out = lax.map(chunk, (jnp.arange(nc), q_c, idx_c))
    return out.transpose(1, 0, 2, 3, 4, 5).reshape(B, S, H, d)


def setup_inputs(seed: int = 0) -> dict:
    key = jax.random.key(seed)
    ks = jax.random.split(key, 7)
    x = jax.random.normal(ks[0], (BATCH, SEQ, D_MODEL), jnp.float32)
    norm_g = 1.0 + 0.02 * jax.random.normal(ks[1], (DEPTH, D_MODEL), jnp.float32)
    w_in_a = jax.random.normal(ks[2], (N_LAYERS_A, D_MODEL, IN_A), jnp.float32) * D_MODEL ** -0.5
    sinks_a = jax.random.normal(ks[3], (N_LAYERS_A, N_HEADS), jnp.float32)
    w_in_b = jax.random.normal(ks[4], (N_LAYERS_B, D_MODEL, IN_B), jnp.float32) * D_MODEL ** -0.5
    w_out = jax.random.normal(ks[5], (DEPTH, MIX_WIDTH, D_MODEL), jnp.float32) * MIX_WIDTH ** -0.5
    final_g = 1.0 + 0.02 * jax.random.normal(ks[6], (D_MODEL,), jnp.float32)
    return {"x": x, "norm_g": norm_g, "w_in_a": w_in_a, "sinks_a": sinks_a,
            "w_in_b": w_in_b, "w_out": w_out, "final_g": final_g}


def reference(x, norm_g, w_in_a, sinks_a, w_in_b, w_out, final_g):
    B, S, _ = x.shape
    cos, sin = rope_tables(S)
    h = x
    for i in range(DEPTH):
        hn = rmsnorm(h, norm_g[i])
        j = i // N_MIXERS
        if i % N_MIXERS == 0:
            q, k, v, z = split_proj(hn, w_in_a[j], N_KV_A)
            q, k = apply_rope(q, cos, sin), apply_rope(k, cos, sin)
            o = sliding_window_sink_attention(q, k, v, sinks_a[j])
        else:
            q, k, v, z = split_proj(hn, w_in_b[j], N_KV_B)
            q, k = apply_rope(q, cos, sin), apply_rope(k, cos, sin)
            o = moba_attention(q, k, v)
        o = o.reshape(B, S, MIX_WIDTH) * jax.nn.silu(z)
        h = h + o @ w_out[i]
    return rmsnorm(h, final_g)
```

```python
import functools
import math

import jax
import jax.numpy as jnp
from jax import lax
from jax.experimental import pallas as pl
from jax.experimental.pallas import tpu as pltpu

D_MODEL = 1024
HEAD_DIM = 64
N_HEADS = 16
MIX_WIDTH = N_HEADS * HEAD_DIM
N_KV_A = 2
N_KV_B = 4
WINDOW = 128
MOBA_BLOCK = 256
MOBA_TOPK = 3
ROPE_THETA = 10000.0
NORM_EPS = 1e-5
NEG = -1e30
LOG2E = 1.4426950408889634

LANES = 128
V_ROWS = HEAD_DIM + 16
TOKEN_TILE = 512
VMEM_LIMIT = 48 * 1024 * 1024

_NT = (((1,), (1,)), ((), ()))


def _dot(a, b):
    return jnp.dot(a, b, preferred_element_type=jnp.float32)


def _dot_nt(a, b):
    return lax.dot_general(a, b, _NT, preferred_element_type=jnp.float32)


def _rope(x, cos, sin_signed, first_half):
    swapped = jnp.where(first_half, pltpu.roll(x, LANES - HEAD_DIM // 2, 1),
                        pltpu.roll(x, HEAD_DIM // 2, 1))
    return x * cos + swapped * sin_signed


def _in_proj_kernel(n_kv, v_blk, h_ref, g_ref, w_ref, cq_ref, sq_ref, ck_ref, sk_ref,
                    q_ref, k_ref, vt_ref, z_ref):
    tm = h_ref.shape[1]
    kvw = n_kv * HEAD_DIM
    x = h_ref[0]
    ms = jnp.mean(x * x, axis=-1, keepdims=True)
    hn = (x * lax.rsqrt(ms + NORM_EPS) * g_ref[...]).astype(jnp.bfloat16)

    lane = lax.broadcasted_iota(jnp.int32, (tm, LANES), 1)
    first_half = (lane % HEAD_DIM) < (HEAD_DIM // 2)

    cq, sq = cq_ref[...], sq_ref[...]
    for c in range(MIX_WIDTH // LANES):
        cols = slice(c * LANES, (c + 1) * LANES)
        q = _dot(hn, w_ref[:, cols])
        q_ref[0, :, cols] = _rope(q, cq, sq, first_half).astype(jnp.bfloat16)

    ck, sk = ck_ref[...], sk_ref[...]
    for c in range(kvw // LANES):
        k = _dot(hn, w_ref[:, MIX_WIDTH + c * LANES:MIX_WIDTH + (c + 1) * LANES])
        k = _rope(k, ck, sk, first_half).astype(jnp.bfloat16)
        k_ref[0, 2 * c] = k[:, :HEAD_DIM]
        k_ref[0, 2 * c + 1] = k[:, HEAD_DIM:]

    ones = jnp.ones((V_ROWS - HEAD_DIM, v_blk), jnp.bfloat16)
    for c in range(kvw // LANES):
        v = _dot(hn, w_ref[:, MIX_WIDTH + kvw + c * LANES:MIX_WIDTH + kvw + (c + 1) * LANES])
        vt = v.T.astype(jnp.bfloat16)
        for hh in range(2):
            for b in range(tm // v_blk):
                vt_ref[0, 2 * c + hh, b, 0:HEAD_DIM, :] = vt[hh * HEAD_DIM:(hh + 1) * HEAD_DIM,
                                                              b * v_blk:(b + 1) * v_blk]
                vt_ref[0, 2 * c + hh, b, HEAD_DIM:V_ROWS, :] = ones

    z0 = MIX_WIDTH + 2 * kvw
    for c in range(MIX_WIDTH // 256):
        z = _dot(hn, w_ref[:, z0 + c * 256:z0 + (c + 1) * 256])
        z_ref[0, :, c * 256:(c + 1) * 256] = (z * (1.0 / (1.0 + jnp.exp(-z)))).astype(jnp.bfloat16)


def _in_proj(h, g, w, tabs, n_kv, v_blk):
    B, S, _ = h.shape
    tm = TOKEN_TILE
    n_in = w.shape[1]
    tab_spec = pl.BlockSpec((tm, LANES), lambda b, t: (t, 0))
    return pl.pallas_call(
        functools.partial(_in_proj_kernel, n_kv, v_blk),
        grid=(B, S // tm),
        in_specs=[
            pl.BlockSpec((1, tm, D_MODEL), lambda b, t: (b, t, 0)),
            pl.BlockSpec((1, D_MODEL), lambda b, t: (0, 0)),
            pl.BlockSpec((D_MODEL, n_in), lambda b, t: (0, 0)),
            tab_spec, tab_spec, tab_spec, tab_spec,
        ],
        out_specs=[
            pl.BlockSpec((1, tm, MIX_WIDTH), lambda b, t: (b, t, 0)),
            pl.BlockSpec((1, n_kv, tm, HEAD_DIM), lambda b, t: (b, 0, t, 0)),
            pl.BlockSpec((1, n_kv, tm // v_blk, V_ROWS, v_blk), lambda b, t: (b, 0, t, 0, 0)),
            pl.BlockSpec((1, tm, MIX_WIDTH), lambda b, t: (b, t, 0)),
        ],
        out_shape=[
            jax.ShapeDtypeStruct((B, S, MIX_WIDTH), jnp.bfloat16),
            jax.ShapeDtypeStruct((B, n_kv, S, HEAD_DIM), jnp.bfloat16),
            jax.ShapeDtypeStruct((B, n_kv, S // v_blk, V_ROWS, v_blk), jnp.bfloat16),
            jax.ShapeDtypeStruct((B, S, MIX_WIDTH), jnp.bfloat16),
        ],
        compiler_params=pltpu.CompilerParams(
            dimension_semantics=("parallel", "parallel"), vmem_limit_bytes=VMEM_LIMIT),
        name="in_proj",
    )(h, g, w, *tabs)


def _out_proj_kernel(final, o_ref, z_ref, h_ref, w_ref, g_ref, out_ref):
    og = o_ref[0] * z_ref[0]
    h = h_ref[0] + _dot(og, w_ref[...])
    if final:
        ms = jnp.mean(h * h, axis=-1, keepdims=True)
        h = h * lax.rsqrt(ms + NORM_EPS) * g_ref[...]
    out_ref[0] = h


def _out_proj(o, sz, h, w, g, final):
    B, S, _ = h.shape
    tm = TOKEN_TILE
    row_spec = pl.BlockSpec((1, tm, D_MODEL), lambda b, t: (b, t, 0))
    return pl.pallas_call(
        functools.partial(_out_proj_kernel, final),
        grid=(B, S // tm),
        in_specs=[row_spec, row_spec, row_spec,
                  pl.BlockSpec((MIX_WIDTH, D_MODEL), lambda b, t: (0, 0)),
                  pl.BlockSpec((1, D_MODEL), lambda b, t: (0, 0))],
        out_specs=row_spec,
        out_shape=jax.ShapeDtypeStruct((B, S, D_MODEL), jnp.float32),
        compiler_params=pltpu.CompilerParams(
            dimension_semantics=("parallel", "parallel"), vmem_limit_bytes=VMEM_LIMIT),
        name="out_proj",
    )(o, sz, h, w, g)


def _stack_heads(q, n):
    return jnp.concatenate([q[:, g * HEAD_DIM:(g + 1) * HEAD_DIM] for g in range(n)], axis=0)


def _unstack_heads_t(ot, n, rows):
    st = jnp.concatenate([ot[:, g * rows:(g + 1) * rows] for g in range(n)], axis=0)
    return st.T


def _swa_kernel(q_ref, k_ref, vt_ref, sink_ref, o_ref):
    n = pl.program_id(2)
    group = q_ref.shape[2] // HEAD_DIM
    rows = group * WINDOW
    qs = _stack_heads(q_ref[0], group)
    first = jnp.maximum(n - 1, 0)
    off = (n - first) * WINDOW
    kb = k_ref[0, 0, pl.ds(pl.multiple_of(first * WINDOW, WINDOW), 2 * WINDOW), :]
    st = _dot_nt(kb, qs)
    key = lax.broadcasted_iota(jnp.int32, (2 * WINDOW, rows), 0)
    qpos = lax.broadcasted_iota(jnp.int32, (2 * WINDOW, rows), 1) % WINDOW
    dist = qpos + off - key
    st = jnp.where((dist >= 0) & (dist < WINDOW), st, NEG)
    sink = sink_ref[0]
    m = jnp.maximum(jnp.max(st, axis=0, keepdims=True), sink)
    p = jnp.exp2(st - m).astype(jnp.bfloat16)
    vt = jnp.concatenate([vt_ref[0, 0, first], vt_ref[0, 0, first + 1]], axis=1)
    acc = _dot(vt, p)
    denom = acc[HEAD_DIM:HEAD_DIM + 1] + jnp.exp2(sink - m)
    ot = acc[:HEAD_DIM] / denom
    o_ref[0] = _unstack_heads_t(ot, group, WINDOW).astype(jnp.bfloat16)


def _swa(q, k, vt, sinks2):
    B, S, _ = q.shape
    n_kv = k.shape[1]
    gw = MIX_WIDTH // n_kv
    return pl.pallas_call(
        _swa_kernel,
        grid=(B, n_kv, S // WINDOW),
        in_specs=[
            pl.BlockSpec((1, WINDOW, gw), lambda b, c, n: (b, n, c)),
            pl.BlockSpec((1, 1, S, HEAD_DIM), lambda b, c, n: (b, c, 0, 0)),
            pl.BlockSpec((1, 1, S // WINDOW, V_ROWS, WINDOW), lambda b, c, n: (b, c, 0, 0, 0)),
            pl.BlockSpec((1, 1, sinks2.shape[2]), lambda b, c, n: (c, 0, 0)),
        ],
        out_specs=pl.BlockSpec((1, WINDOW, gw), lambda b, c, n: (b, n, c)),
        out_shape=jax.ShapeDtypeStruct((B, S, MIX_WIDTH), jnp.bfloat16),
        compiler_params=pltpu.CompilerParams(
            dimension_semantics=("parallel", "parallel", "arbitrary"),
            vmem_limit_bytes=VMEM_LIMIT),
        name="swa",
    )(q, k, vt, sinks2)


def _moba_kernel(q_ref, k_ref, vt_ref, o_ref, km_ref, qs_ref, bias_ref, m_ref, acc_ref):
    i = pl.program_id(2)
    nb = k_ref.shape[2] // MOBA_BLOCK
    group = q_ref.shape[2] // HEAD_DIM
    rows = group * MOBA_BLOCK

    @pl.when(i == 0)
    def _():
        means = [jnp.sum(k_ref[0, 0, j * MOBA_BLOCK:(j + 1) * MOBA_BLOCK, :].astype(jnp.float32),
                         axis=0, keepdims=True) for j in range(nb)]
        km = jnp.concatenate(means, axis=0) * (1.0 / MOBA_BLOCK)
        hi = km.astype(jnp.bfloat16)
        km_ref[0:nb] = hi
        km_ref[nb:2 * nb] = (km - hi.astype(jnp.float32)).astype(jnp.bfloat16)

    qs_ref[...] = _stack_heads(q_ref[0], group)
    qs = qs_ref[...]

    gt = _dot_nt(km_ref[...], qs)
    gate = gt[0:nb] + gt[nb:2 * nb]
    blk = lax.broadcasted_iota(jnp.int32, (nb, rows), 0)
    past = blk < i
    gate = jnp.where(past, gate, NEG)
    rank = jnp.zeros((nb, rows), jnp.int32)
    for jp in range(nb):
        gj = gate[jp:jp + 1, :]
        beats = (gj > gate) | ((gj == gate) & (jp < blk))
        rank = rank + beats.astype(jnp.int32)
    sel = past & (rank < MOBA_TOPK)
    bias_ref[...] = jnp.where(sel, 0.0, NEG)

    kj = k_ref[0, 0, pl.ds(pl.multiple_of(i * MOBA_BLOCK, MOBA_BLOCK), MOBA_BLOCK), :]
    st = _dot_nt(kj, qs)
    key = lax.broadcasted_iota(jnp.int32, (MOBA_BLOCK, rows), 0)
    qpos = lax.broadcasted_iota(jnp.int32, (MOBA_BLOCK, rows), 1) % MOBA_BLOCK
    st = jnp.where(key <= qpos, st, NEG)
    m0 = jnp.max(st, axis=0, keepdims=True)
    p = jnp.exp2(st - m0).astype(jnp.bfloat16)
    m_ref[...] = m0
    acc_ref[...] = _dot(vt_ref[0, 0, i], p)

    def body(j, carry):
        kj = k_ref[0, 0, pl.ds(pl.multiple_of(j * MOBA_BLOCK, MOBA_BLOCK), MOBA_BLOCK), :]
        st = _dot_nt(kj, qs_ref[...])
        bias = bias_ref[pl.ds(j, 1), :]
        m_old = m_ref[...]
        m_new = jnp.maximum(m_old, jnp.max(st, axis=0, keepdims=True) + bias)
        alpha = jnp.exp2(m_old - m_new)
        p = jnp.exp2(st - (m_new - bias)).astype(jnp.bfloat16)
        acc_ref[...] = acc_ref[...] * alpha + _dot(vt_ref[0, 0, j], p)
        m_ref[...] = m_new
        return carry

    lax.fori_loop(0, i, body, 0)

    acc = acc_ref[...]
    ot = acc[:HEAD_DIM] / acc[HEAD_DIM:HEAD_DIM + 1]
    o_ref[0] = _unstack_heads_t(ot, group, MOBA_BLOCK).astype(jnp.bfloat16)


def _moba(q, k, vt):
    B, S, _ = q.shape
    n_kv = k.shape[1]
    gw = MIX_WIDTH // n_kv
    nb = S // MOBA_BLOCK
    rows = (gw // HEAD_DIM) * MOBA_BLOCK
    return pl.pallas_call(
        _moba_kernel,
        grid=(B, n_kv, nb),
        in_specs=[
            pl.BlockSpec((1, MOBA_BLOCK, gw), lambda b, c, i: (b, i, c)),
            pl.BlockSpec((1, 1, S, HEAD_DIM), lambda b, c, i: (b, c, 0, 0)),
            pl.BlockSpec((1, 1, nb, V_ROWS, MOBA_BLOCK), lambda b, c, i: (b, c, 0, 0, 0)),
        ],
        out_specs=pl.BlockSpec((1, MOBA_BLOCK, gw), lambda b, c, i: (b, i, c)),
        out_shape=jax.ShapeDtypeStruct((B, S, MIX_WIDTH), jnp.bfloat16),
        scratch_shapes=[
            pltpu.VMEM((2 * nb, HEAD_DIM), jnp.bfloat16),
            pltpu.VMEM((rows, HEAD_DIM), jnp.bfloat16),
            pltpu.VMEM((nb, rows), jnp.float32),
            pltpu.VMEM((1, rows), jnp.float32),
            pltpu.VMEM((V_ROWS, rows), jnp.float32),
        ],
        compiler_params=pltpu.CompilerParams(
            dimension_semantics=("parallel", "parallel", "arbitrary"),
            vmem_limit_bytes=VMEM_LIMIT),
        name="moba",
    )(q, k, vt)


def _rope_tables(seq, q_scale):
    pos = jnp.arange(seq, dtype=jnp.float32)
    inv = ROPE_THETA ** (-jnp.arange(0, HEAD_DIM, 2, dtype=jnp.float32) / HEAD_DIM)
    ang = pos[:, None] * inv[None, :]
    cos, sin = jnp.cos(ang), jnp.sin(ang)
    ck = jnp.tile(cos, (1, LANES // (HEAD_DIM // 2)))
    sk = jnp.tile(jnp.concatenate([-sin, sin], axis=1), (1, LANES // HEAD_DIM))
    return ck * q_scale, sk * q_scale, ck, sk


def kernel(x, norm_g, w_in_a, sinks_a, w_in_b, w_out, final_g):
    B, S, _ = x.shape
    depth = norm_g.shape[0]
    assert S % TOKEN_TILE == 0 and S % MOBA_BLOCK == 0
    tabs = _rope_tables(S, LOG2E / math.sqrt(HEAD_DIM))
    h = x
    for i in range(depth):
        j = i // 2
        g = norm_g[i].reshape(1, D_MODEL)
        w_o = w_out[i].astype(jnp.bfloat16)
        if i % 2 == 0:
            q, k, vt, sz = _in_proj(h, g, w_in_a[j].astype(jnp.bfloat16), tabs, N_KV_A, WINDOW)
            group = N_HEADS // N_KV_A
            sinks2 = jnp.repeat(sinks_a[j].astype(jnp.float32) * LOG2E, WINDOW).reshape(
                N_KV_A, 1, group * WINDOW)
            o = _swa(q, k, vt, sinks2)
        else:
            q, k, vt, sz = _in_proj(h, g, w_in_b[j].astype(jnp.bfloat16), tabs, N_KV_B, MOBA_BLOCK)
            o = _moba(q, k, vt)
        final = i == depth - 1
        h = _out_proj(o, sz, h, w_o, final_g.reshape(1, D_MODEL), final)
    return h
```

```python
import functools
import math

import jax
import jax.numpy as jnp
from jax import lax
from jax.experimental import pallas as pl
from jax.experimental.pallas import tpu as pltpu

D_MODEL = 1024
HEAD_DIM = 64
N_HEADS = 16
MIX_WIDTH = N_HEADS * HEAD_DIM
N_KV_A = 2
N_KV_B = 4
WINDOW = 128
MOBA_BLOCK = 256
MOBA_TOPK = 3
ROPE_THETA = 10000.0
NORM_EPS = 1e-5
NEG = -1e30
LOG2E = 1.4426950408889634

LANES = 128
V_ROWS = HEAD_DIM + 16
TOKEN_TILE = 512
SWA_Q_TILE = 512
VMEM_LIMIT = 48 * 1024 * 1024

_NT = (((1,), (1,)), ((), ()))


def _dot(a, b):
    return jnp.dot(a, b, preferred_element_type=jnp.float32)


def _dot_nt(a, b):
    return lax.dot_general(a, b, _NT, preferred_element_type=jnp.float32)


def _rope(x, cos, sin_signed, first_half):
    swapped = jnp.where(first_half, pltpu.roll(x, LANES - HEAD_DIM // 2, 1),
                        pltpu.roll(x, HEAD_DIM // 2, 1))
    return x * cos + swapped * sin_signed


def _in_proj_kernel(n_kv, v_blk, h_ref, g_ref, w_ref, cq_ref, sq_ref, ck_ref, sk_ref,
                    q_ref, k_ref, vt_ref, z_ref):
    tm = h_ref.shape[1]
    kvw = n_kv * HEAD_DIM
    x = h_ref[0]
    ms = jnp.mean(x * x, axis=-1, keepdims=True)
    hn = (x * lax.rsqrt(ms + NORM_EPS) * g_ref[...]).astype(jnp.bfloat16)

    lane = lax.broadcasted_iota(jnp.int32, (tm, LANES), 1)
    first_half = (lane % HEAD_DIM) < (HEAD_DIM // 2)

    cq, sq = cq_ref[...], sq_ref[...]
    for c in range(MIX_WIDTH // LANES):
        cols = slice(c * LANES, (c + 1) * LANES)
        q = _dot(hn, w_ref[:, cols])
        q_ref[0, :, cols] = _rope(q, cq, sq, first_half).astype(jnp.bfloat16)

    ck, sk = ck_ref[...], sk_ref[...]
    for c in range(kvw // LANES):
        k = _dot(hn, w_ref[:, MIX_WIDTH + c * LANES:MIX_WIDTH + (c + 1) * LANES])
        k = _rope(k, ck, sk, first_half).astype(jnp.bfloat16)
        k_ref[0, 2 * c] = k[:, :HEAD_DIM]
        k_ref[0, 2 * c + 1] = k[:, HEAD_DIM:]

    ones = jnp.ones((V_ROWS - HEAD_DIM, v_blk), jnp.bfloat16)
    for c in range(kvw // LANES):
        v = _dot(hn, w_ref[:, MIX_WIDTH + kvw + c * LANES:MIX_WIDTH + kvw + (c + 1) * LANES])
        vt = v.T.astype(jnp.bfloat16)
        for hh in range(2):
            for b in range(tm // v_blk):
                vt_ref[0, 2 * c + hh, b, 0:HEAD_DIM, :] = vt[hh * HEAD_DIM:(hh + 1) * HEAD_DIM,
                                                              b * v_blk:(b + 1) * v_blk]
                vt_ref[0, 2 * c + hh, b, HEAD_DIM:V_ROWS, :] = ones

    z0 = MIX_WIDTH + 2 * kvw
    for c in range(MIX_WIDTH // 256):
        z = _dot(hn, w_ref[:, z0 + c * 256:z0 + (c + 1) * 256])
        z_ref[0, :, c * 256:(c + 1) * 256] = (z * (1.0 / (1.0 + jnp.exp(-z)))).astype(jnp.bfloat16)


def _in_proj(h, g, w, tabs, n_kv, v_blk):
    B, S, _ = h.shape
    tm = TOKEN_TILE
    n_in = w.shape[1]
    tab_spec = pl.BlockSpec((tm, LANES), lambda b, t: (t, 0))
    return pl.pallas_call(
        functools.partial(_in_proj_kernel, n_kv, v_blk),
        grid=(B, S // tm),
        in_specs=[
            pl.BlockSpec((1, tm, D_MODEL), lambda b, t: (b, t, 0)),
            pl.BlockSpec((1, D_MODEL), lambda b, t: (0, 0)),
            pl.BlockSpec((D_MODEL, n_in), lambda b, t: (0, 0)),
            tab_spec, tab_spec, tab_spec, tab_spec,
        ],
        out_specs=[
            pl.BlockSpec((1, tm, MIX_WIDTH), lambda b, t: (b, t, 0)),
            pl.BlockSpec((1, n_kv, tm, HEAD_DIM), lambda b, t: (b, 0, t, 0)),
            pl.BlockSpec((1, n_kv, tm // v_blk, V_ROWS, v_blk), lambda b, t: (b, 0, t, 0, 0)),
            pl.BlockSpec((1, tm, MIX_WIDTH), lambda b, t: (b, t, 0)),
        ],
        out_shape=[
            jax.ShapeDtypeStruct((B, S, MIX_WIDTH), jnp.bfloat16),
            jax.ShapeDtypeStruct((B, n_kv, S, HEAD_DIM), jnp.bfloat16),
            jax.ShapeDtypeStruct((B, n_kv, S // v_blk, V_ROWS, v_blk), jnp.bfloat16),
            jax.ShapeDtypeStruct((B, S, MIX_WIDTH), jnp.bfloat16),
        ],
        compiler_params=pltpu.CompilerParams(
            dimension_semantics=("parallel", "parallel"), vmem_limit_bytes=VMEM_LIMIT),
        name="in_proj",
    )(h, g, w, *tabs)


def _out_proj_kernel(final, o_ref, z_ref, h_ref, w_ref, g_ref, out_ref):
    og = o_ref[0] * z_ref[0]
    h = h_ref[0] + _dot(og, w_ref[...])
    if final:
        ms = jnp.mean(h * h, axis=-1, keepdims=True)
        h = h * lax.rsqrt(ms + NORM_EPS) * g_ref[...]
    out_ref[0] = h


def _out_proj(o, sz, h, w, g, final):
    B, S, _ = h.shape
    tm = TOKEN_TILE
    row_spec = pl.BlockSpec((1, tm, D_MODEL), lambda b, t: (b, t, 0))
    return pl.pallas_call(
        functools.partial(_out_proj_kernel, final),
        grid=(B, S // tm),
        in_specs=[row_spec, row_spec, row_spec,
                  pl.BlockSpec((MIX_WIDTH, D_MODEL), lambda b, t: (0, 0)),
                  pl.BlockSpec((1, D_MODEL), lambda b, t: (0, 0))],
        out_specs=row_spec,
        out_shape=jax.ShapeDtypeStruct((B, S, D_MODEL), jnp.float32),
        compiler_params=pltpu.CompilerParams(
            dimension_semantics=("parallel", "parallel"), vmem_limit_bytes=VMEM_LIMIT),
        name="out_proj",
    )(o, sz, h, w, g)


def _stack_heads(q, n):
    return jnp.concatenate([q[:, g * HEAD_DIM:(g + 1) * HEAD_DIM] for g in range(n)], axis=0)


def _unstack_heads_t(ot, n, rows):
    st = jnp.concatenate([ot[:, g * rows:(g + 1) * rows] for g in range(n)], axis=0)
    return st.T


def _swa_kernel(q_ref, k_ref, vt_ref, sink_ref, mask_ref, o_ref):
    n = pl.program_id(2)
    group = q_ref.shape[2] // HEAD_DIM
    n_sub = q_ref.shape[1] // WINDOW
    sink = sink_ref[0]
    st, mx, p, m = {}, {}, {}, {}

    def band(sb):
        qb = n * n_sub + sb
        return qb, jnp.maximum(qb - 1, 0)

    def scores(sb):
        qb, first = band(sb)
        qs = _stack_heads(q_ref[0, sb * WINDOW:(sb + 1) * WINDOW, :], group)
        kb = k_ref[0, 0, pl.ds(pl.multiple_of(first * WINDOW, WINDOW), 2 * WINDOW), :]
        st[sb] = _dot_nt(kb, qs) + mask_ref[jnp.where(qb == 0, 1, 0)]
        mx[sb] = jnp.max(st[sb], axis=0, keepdims=True)

    def probs(sb):
        m[sb] = jnp.maximum(mx[sb], sink)
        p[sb] = jnp.exp2(st[sb] - m[sb]).astype(jnp.bfloat16)

    def output(sb):
        _, first = band(sb)
        vt = jnp.concatenate([vt_ref[0, 0, first], vt_ref[0, 0, first + 1]], axis=1)
        acc = _dot(vt, p[sb])
        denom = acc[HEAD_DIM:HEAD_DIM + 1] + jnp.exp2(sink - m[sb])
        ot = acc[:HEAD_DIM] / denom
        o_ref[0, sb * WINDOW:(sb + 1) * WINDOW, :] = _unstack_heads_t(
            ot, group, WINDOW).astype(jnp.bfloat16)

    for slot in range(n_sub + 2):
        if slot < n_sub:
            scores(slot)
        if 0 <= slot - 1 < n_sub:
            probs(slot - 1)
        if 0 <= slot - 2 < n_sub:
            output(slot - 2)


def _swa_mask(rows):
    key = lax.broadcasted_iota(jnp.int32, (2 * WINDOW, rows), 0)
    qpos = lax.broadcasted_iota(jnp.int32, (2 * WINDOW, rows), 1) % WINDOW
    masks = []
    for off in (WINDOW, 0):
        dist = qpos + off - key
        masks.append(jnp.where((dist >= 0) & (dist < WINDOW), 0.0, NEG))
    return jnp.stack(masks).astype(jnp.float32)


def _swa(q, k, vt, sinks2):
    B, S, _ = q.shape
    n_kv = k.shape[1]
    gw = MIX_WIDTH // n_kv
    rows = (gw // HEAD_DIM) * WINDOW
    tq = SWA_Q_TILE
    return pl.pallas_call(
        _swa_kernel,
        grid=(B, n_kv, S // tq),
        in_specs=[
            pl.BlockSpec((1, tq, gw), lambda b, c, n: (b, n, c)),
            pl.BlockSpec((1, 1, S, HEAD_DIM), lambda b, c, n: (b, c, 0, 0)),
            pl.BlockSpec((1, 1, S // WINDOW, V_ROWS, WINDOW), lambda b, c, n: (b, c, 0, 0, 0)),
            pl.BlockSpec((1, 1, rows), lambda b, c, n: (c, 0, 0)),
            pl.BlockSpec((2, 2 * WINDOW, rows), lambda b, c, n: (0, 0, 0)),
        ],
        out_specs=pl.BlockSpec((1, tq, gw), lambda b, c, n: (b, n, c)),
        out_shape=jax.ShapeDtypeStruct((B, S, MIX_WIDTH), jnp.bfloat16),
        compiler_params=pltpu.CompilerParams(
            dimension_semantics=("parallel", "parallel", "arbitrary"),
            vmem_limit_bytes=VMEM_LIMIT),
        name="swa",
    )(q, k, vt, sinks2, _swa_mask(rows))


def _moba_kernel(q_ref, k_ref, vt_ref, o_ref, km_ref, qs_ref, bias_ref, s0_ref, s1_ref,
                 m_ref, acc_ref):
    i = pl.program_id(2)
    nb = k_ref.shape[2] // MOBA_BLOCK
    group = q_ref.shape[2] // HEAD_DIM
    rows = group * MOBA_BLOCK

    @pl.when(i == 0)
    def _():
        means = [jnp.sum(k_ref[0, 0, j * MOBA_BLOCK:(j + 1) * MOBA_BLOCK, :].astype(jnp.float32),
                         axis=0, keepdims=True) for j in range(nb)]
        km = jnp.concatenate(means, axis=0) * (1.0 / MOBA_BLOCK)
        hi = km.astype(jnp.bfloat16)
        km_ref[0:nb] = hi
        km_ref[nb:2 * nb] = (km - hi.astype(jnp.float32)).astype(jnp.bfloat16)

    def scores(j, s_ref):
        kj = k_ref[0, 0, pl.ds(pl.multiple_of(j * MOBA_BLOCK, MOBA_BLOCK), MOBA_BLOCK), :]
        s_ref[...] = _dot_nt(kj, qs_ref[...])

    def accumulate(j, s_ref, m_old, acc):
        bias = bias_ref[pl.ds(j, 1), :]
        m_new = jnp.maximum(m_old, jnp.max(s_ref[...], axis=0, keepdims=True) + bias)
        alpha = jnp.exp2(m_old - m_new)
        p = jnp.exp2(s_ref[...] - (m_new - bias)).astype(jnp.bfloat16)
        return m_new, acc * alpha + _dot(vt_ref[0, 0, j], p)

    qs_ref[...] = _stack_heads(q_ref[0], group)
    qs = qs_ref[...]
    scores(0, s0_ref)

    gt = _dot_nt(km_ref[...], qs)
    gate = gt[0:nb] + gt[nb:2 * nb]
    blk = lax.broadcasted_iota(jnp.int32, (nb, rows), 0)
    past = blk < i
    gate = jnp.where(past, gate, NEG)
    rank = jnp.zeros((nb, rows), jnp.int32)
    for jp in range(nb):
        gj = gate[jp:jp + 1, :]
        beats = (gj > gate) | ((gj == gate) & (jp < blk))
        rank = rank + beats.astype(jnp.int32)
    sel = past & (rank < MOBA_TOPK)
    bias_ref[...] = jnp.where(sel, 0.0, NEG)

    kj = k_ref[0, 0, pl.ds(pl.multiple_of(i * MOBA_BLOCK, MOBA_BLOCK), MOBA_BLOCK), :]
    st = _dot_nt(kj, qs)
    key = lax.broadcasted_iota(jnp.int32, (MOBA_BLOCK, rows), 0)
    qpos = lax.broadcasted_iota(jnp.int32, (MOBA_BLOCK, rows), 1) % MOBA_BLOCK
    st = jnp.where(key <= qpos, st, NEG)
    m0 = jnp.max(st, axis=0, keepdims=True)
    p = jnp.exp2(st - m0).astype(jnp.bfloat16)
    m_ref[...] = m0
    acc_ref[...] = _dot(vt_ref[0, 0, i], p)

    def pair(u, carry):
        t = 2 * u
        scores(t + 1, s1_ref)
        m, acc = accumulate(t, s0_ref, m_ref[...], acc_ref[...])
        scores(jnp.minimum(t + 2, i), s0_ref)
        m, acc = accumulate(t + 1, s1_ref, m, acc)
        m_ref[...] = m
        acc_ref[...] = acc
        return carry

    lax.fori_loop(0, (i + 1) // 2, pair, 0)

    acc = acc_ref[...]
    ot = acc[:HEAD_DIM] / acc[HEAD_DIM:HEAD_DIM + 1]
    o_ref[0] = _unstack_heads_t(ot, group, MOBA_BLOCK).astype(jnp.bfloat16)


def _moba(q, k, vt):
    B, S, _ = q.shape
    n_kv = k.shape[1]
    gw = MIX_WIDTH // n_kv
    nb = S // MOBA_BLOCK
    rows = (gw // HEAD_DIM) * MOBA_BLOCK
    return pl.pallas_call(
        _moba_kernel,
        grid=(B, n_kv, nb),
        in_specs=[
            pl.BlockSpec((1, MOBA_BLOCK, gw), lambda b, c, i: (b, i, c)),
            pl.BlockSpec((1, 1, S, HEAD_DIM), lambda b, c, i: (b, c, 0, 0)),
            pl.BlockSpec((1, 1, nb, V_ROWS, MOBA_BLOCK), lambda b, c, i: (b, c, 0, 0, 0)),
        ],
        out_specs=pl.BlockSpec((1, MOBA_BLOCK, gw), lambda b, c, i: (b, i, c)),
        out_shape=jax.ShapeDtypeStruct((B, S, MIX_WIDTH), jnp.bfloat16),
        scratch_shapes=[
            pltpu.VMEM((2 * nb, HEAD_DIM), jnp.bfloat16),
            pltpu.VMEM((rows, HEAD_DIM), jnp.bfloat16),
            pltpu.VMEM((nb, rows), jnp.float32),
            pltpu.VMEM((MOBA_BLOCK, rows), jnp.float32),
            pltpu.VMEM((MOBA_BLOCK, rows), jnp.float32),
            pltpu.VMEM((1, rows), jnp.float32),
            pltpu.VMEM((V_ROWS, rows), jnp.float32),
        ],
        compiler_params=pltpu.CompilerParams(
            dimension_semantics=("parallel", "parallel", "arbitrary"),
            vmem_limit_bytes=VMEM_LIMIT),
        name="moba",
    )(q, k, vt)


def _rope_tables(seq, q_scale):
    pos = jnp.arange(seq, dtype=jnp.float32)
    inv = ROPE_THETA ** (-jnp.arange(0, HEAD_DIM, 2, dtype=jnp.float32) / HEAD_DIM)
    ang = pos[:, None] * inv[None, :]
    cos, sin = jnp.cos(ang), jnp.sin(ang)
    ck = jnp.tile(cos, (1, LANES // (HEAD_DIM // 2)))
    sk = jnp.tile(jnp.concatenate([-sin, sin], axis=1), (1, LANES // HEAD_DIM))
    return ck * q_scale, sk * q_scale, ck, sk


def kernel(x, norm_g, w_in_a, sinks_a, w_in_b, w_out, final_g):
    B, S, _ = x.shape
    depth = norm_g.shape[0]
    assert S % TOKEN_TILE == 0 and S % MOBA_BLOCK == 0
    tabs = _rope_tables(S, LOG2E / math.sqrt(HEAD_DIM))
    h = x
    for i in range(depth):
        j = i // 2
        g = norm_g[i].reshape(1, D_MODEL)
        w_o = w_out[i].astype(jnp.bfloat16)
        if i % 2 == 0:
            q, k, vt, sz = _in_proj(h, g, w_in_a[j].astype(jnp.bfloat16), tabs, N_KV_A, WINDOW)
            group = N_HEADS // N_KV_A
            sinks2 = jnp.repeat(sinks_a[j].astype(jnp.float32) * LOG2E, WINDOW).reshape(
                N_KV_A, 1, group * WINDOW)
            o = _swa(q, k, vt, sinks2)
        else:
            q, k, vt, sz = _in_proj(h, g, w_in_b[j].astype(jnp.bfloat16), tabs, N_KV_B, MOBA_BLOCK)
            o = _moba(q, k, vt)
        final = i == depth - 1
        h = _out_proj(o, sz, h, w_o, final_g.reshape(1, D_MODEL), final)
    return h
```

```python
import functools
import math

import jax
import jax.numpy as jnp
from jax import lax
from jax.experimental import pallas as pl
from jax.experimental.pallas import tpu as pltpu

D_MODEL = 1024
HEAD_DIM = 64
N_HEADS = 16
MIX_WIDTH = N_HEADS * HEAD_DIM
N_KV_A = 2
N_KV_B = 4
WINDOW = 128
MOBA_BLOCK = 256
MOBA_TOPK = 3
ROPE_THETA = 10000.0
NORM_EPS = 1e-5
NEG = -1e30
BELOW_NEG = -3e38
LOG2E = 1.4426950408889634

LANES = 128
MXU_COLS = 256
V_ROWS = HEAD_DIM + 16
TOKEN_TILE = 512
SWA_Q_TILE = 512
VMEM_LIMIT = 48 * 1024 * 1024

_NT = (((1,), (1,)), ((), ()))


def _dot(a, b):
    return jnp.dot(a, b, preferred_element_type=jnp.float32)


def _dot_nt(a, b):
    return lax.dot_general(a, b, _NT, preferred_element_type=jnp.float32)


def _rope(x, cos, sin_signed, first_half):
    swapped = jnp.where(first_half, pltpu.roll(x, LANES - HEAD_DIM // 2, 1),
                        pltpu.roll(x, HEAD_DIM // 2, 1))
    return x * cos + swapped * sin_signed


def _project(x, g_ref, w_ref, tab_refs, q_ref, k_ref, vt_ref, z_ref, n_kv, v_blk):
    tm = x.shape[0]
    kvw = n_kv * HEAD_DIM
    ms = jnp.mean(x * x, axis=-1, keepdims=True)
    hn = (x * lax.rsqrt(ms + NORM_EPS) * g_ref[...]).astype(jnp.bfloat16)

    lane = lax.broadcasted_iota(jnp.int32, (tm, LANES), 1)
    first_half = (lane % HEAD_DIM) < (HEAD_DIM // 2)
    cq, sq, ck, sk = (t[...] for t in tab_refs)

    for c in range(MIX_WIDTH // MXU_COLS):
        q = _dot(hn, w_ref[:, c * MXU_COLS:(c + 1) * MXU_COLS])
        for hh in range(MXU_COLS // LANES):
            lo = c * MXU_COLS + hh * LANES
            q_ref[0, :, lo:lo + LANES] = _rope(
                q[:, hh * LANES:(hh + 1) * LANES], cq, sq, first_half).astype(jnp.bfloat16)

    kv = _dot(hn, w_ref[:, MIX_WIDTH:MIX_WIDTH + 2 * kvw])
    for c in range(kvw // LANES):
        k = _rope(kv[:, c * LANES:(c + 1) * LANES], ck, sk, first_half).astype(jnp.bfloat16)
        k_ref[0, 2 * c] = k[:, :HEAD_DIM]
        k_ref[0, 2 * c + 1] = k[:, HEAD_DIM:]

    ones = jnp.ones((V_ROWS - HEAD_DIM, v_blk), jnp.bfloat16)
    for c in range(kvw // LANES):
        vt = kv[:, kvw + c * LANES:kvw + (c + 1) * LANES].T.astype(jnp.bfloat16)
        for hh in range(2):
            for b in range(tm // v_blk):
                vt_ref[0, 2 * c + hh, b, 0:HEAD_DIM, :] = vt[hh * HEAD_DIM:(hh + 1) * HEAD_DIM,
                                                              b * v_blk:(b + 1) * v_blk]
                vt_ref[0, 2 * c + hh, b, HEAD_DIM:V_ROWS, :] = ones

    z0 = MIX_WIDTH + 2 * kvw
    for c in range(MIX_WIDTH // MXU_COLS):
        z = _dot(hn, w_ref[:, z0 + c * MXU_COLS:z0 + (c + 1) * MXU_COLS])
        z_ref[0, :, c * MXU_COLS:(c + 1) * MXU_COLS] = (
            z * (1.0 / (1.0 + jnp.exp(-z)))).astype(jnp.bfloat16)


def _in_proj_kernel(n_kv, v_blk, h_ref, g_ref, w_ref, cq_ref, sq_ref, ck_ref, sk_ref,
                    q_ref, k_ref, vt_ref, z_ref):
    _project(h_ref[0], g_ref, w_ref, (cq_ref, sq_ref, ck_ref, sk_ref),
             q_ref, k_ref, vt_ref, z_ref, n_kv, v_blk)


def _mid_kernel(n_kv, v_blk, o_ref, sz_ref, h_ref, wo_ref, g_ref, w_ref,
                cq_ref, sq_ref, ck_ref, sk_ref, hout_ref, q_ref, k_ref, vt_ref, z_ref):
    h = h_ref[0] + _dot(o_ref[0] * sz_ref[0], wo_ref[...])
    hout_ref[0] = h
    _project(h, g_ref, w_ref, (cq_ref, sq_ref, ck_ref, sk_ref),
             q_ref, k_ref, vt_ref, z_ref, n_kv, v_blk)


def _proj_out_specs(B, S, tm, n_kv, v_blk):
    specs = [
        pl.BlockSpec((1, tm, MIX_WIDTH), lambda b, t: (b, t, 0)),
        pl.BlockSpec((1, n_kv, tm, HEAD_DIM), lambda b, t: (b, 0, t, 0)),
        pl.BlockSpec((1, n_kv, tm // v_blk, V_ROWS, v_blk), lambda b, t: (b, 0, t, 0, 0)),
        pl.BlockSpec((1, tm, MIX_WIDTH), lambda b, t: (b, t, 0)),
    ]
    shapes = [
        jax.ShapeDtypeStruct((B, S, MIX_WIDTH), jnp.bfloat16),
        jax.ShapeDtypeStruct((B, n_kv, S, HEAD_DIM), jnp.bfloat16),
        jax.ShapeDtypeStruct((B, n_kv, S // v_blk, V_ROWS, v_blk), jnp.bfloat16),
        jax.ShapeDtypeStruct((B, S, MIX_WIDTH), jnp.bfloat16),
    ]
    return specs, shapes


def _resident(shape):
    return pl.BlockSpec(shape, lambda b, t: (0,) * len(shape), pipeline_mode=pl.Buffered(1))


def _in_proj(h, g, w, tabs, n_kv, v_blk):
    B, S, _ = h.shape
    tm = TOKEN_TILE
    tab_spec = pl.BlockSpec((tm, LANES), lambda b, t: (t, 0))
    out_specs, out_shape = _proj_out_specs(B, S, tm, n_kv, v_blk)
    return pl.pallas_call(
        functools.partial(_in_proj_kernel, n_kv, v_blk),
        grid=(B, S // tm),
        in_specs=[
            pl.BlockSpec((1, tm, D_MODEL), lambda b, t: (b, t, 0)),
            _resident((1, D_MODEL)),
            _resident(w.shape),
            tab_spec, tab_spec, tab_spec, tab_spec,
        ],
        out_specs=out_specs,
        out_shape=out_shape,
        compiler_params=pltpu.CompilerParams(
            dimension_semantics=("parallel", "parallel"), vmem_limit_bytes=VMEM_LIMIT),
        name="in_proj",
    )(h, g, w, *tabs)


def _mid(o, sz, h, w_o, g, w, tabs, n_kv, v_blk):
    B, S, _ = h.shape
    tm = TOKEN_TILE
    row_spec = pl.BlockSpec((1, tm, D_MODEL), lambda b, t: (b, t, 0))
    tab_spec = pl.BlockSpec((tm, LANES), lambda b, t: (t, 0))
    out_specs, out_shape = _proj_out_specs(B, S, tm, n_kv, v_blk)
    return pl.pallas_call(
        functools.partial(_mid_kernel, n_kv, v_blk),
        grid=(B, S // tm),
        in_specs=[
            row_spec, row_spec, row_spec,
            _resident(w_o.shape),
            _resident((1, D_MODEL)),
            _resident(w.shape),
            tab_spec, tab_spec, tab_spec, tab_spec,
        ],
        out_specs=[row_spec] + out_specs,
        out_shape=[jax.ShapeDtypeStruct((B, S, D_MODEL), jnp.float32)] + out_shape,
        compiler_params=pltpu.CompilerParams(
            dimension_semantics=("parallel", "parallel"), vmem_limit_bytes=VMEM_LIMIT),
        name="mid",
    )(o, sz, h, w_o, g, w, *tabs)


def _out_proj_kernel(o_ref, z_ref, h_ref, w_ref, g_ref, out_ref):
    h = h_ref[0] + _dot(o_ref[0] * z_ref[0], w_ref[...])
    ms = jnp.mean(h * h, axis=-1, keepdims=True)
    out_ref[0] = h * lax.rsqrt(ms + NORM_EPS) * g_ref[...]


def _out_proj(o, sz, h, w, g):
    B, S, _ = h.shape
    tm = TOKEN_TILE
    row_spec = pl.BlockSpec((1, tm, D_MODEL), lambda b, t: (b, t, 0))
    return pl.pallas_call(
        _out_proj_kernel,
        grid=(B, S // tm),
        in_specs=[row_spec, row_spec, row_spec, _resident(w.shape), _resident((1, D_MODEL))],
        out_specs=row_spec,
        out_shape=jax.ShapeDtypeStruct((B, S, D_MODEL), jnp.float32),
        compiler_params=pltpu.CompilerParams(
            dimension_semantics=("parallel", "parallel"), vmem_limit_bytes=VMEM_LIMIT),
        name="out_proj",
    )(o, sz, h, w, g)


def _stack_heads(q, n):
    return jnp.concatenate([q[:, g * HEAD_DIM:(g + 1) * HEAD_DIM] for g in range(n)], axis=0)


def _unstack_heads_t(ot, n, rows):
    st = jnp.concatenate([ot[:, g * rows:(g + 1) * rows] for g in range(n)], axis=0)
    return st.T


def _swa_kernel(q_ref, k_ref, vt_ref, sink_ref, mask_ref, o_ref):
    n = pl.program_id(2)
    group = q_ref.shape[2] // HEAD_DIM
    n_sub = q_ref.shape[1] // WINDOW
    sink = sink_ref[0]
    st, mx, p, m = {}, {}, {}, {}

    def band(sb):
        qb = n * n_sub + sb
        return qb, jnp.maximum(qb - 1, 0)

    def scores(sb):
        qb, first = band(sb)
        qs = _stack_heads(q_ref[0, sb * WINDOW:(sb + 1) * WINDOW, :], group)
        kb = k_ref[0, 0, pl.ds(pl.multiple_of(first * WINDOW, WINDOW), 2 * WINDOW), :]
        st[sb] = _dot_nt(kb, qs) + mask_ref[jnp.where(qb == 0, 1, 0)]
        mx[sb] = jnp.max(st[sb], axis=0, keepdims=True)

    def probs(sb):
        m[sb] = jnp.maximum(mx[sb], sink)
        p[sb] = jnp.exp2(st[sb] - m[sb]).astype(jnp.bfloat16)

    def output(sb):
        _, first = band(sb)
        vt = jnp.concatenate([vt_ref[0, 0, first], vt_ref[0, 0, first + 1]], axis=1)
        acc = _dot(vt, p[sb])
        denom = acc[HEAD_DIM:HEAD_DIM + 1] + jnp.exp2(sink - m[sb])
        ot = acc[:HEAD_DIM] / denom
        o_ref[0, sb * WINDOW:(sb + 1) * WINDOW, :] = _unstack_heads_t(
            ot, group, WINDOW).astype(jnp.bfloat16)

    for slot in range(n_sub + 2):
        if slot < n_sub:
            scores(slot)
        if 0 <= slot - 1 < n_sub:
            probs(slot - 1)
        if 0 <= slot - 2 < n_sub:
            output(slot - 2)


def _swa_mask(rows):
    key = lax.broadcasted_iota(jnp.int32, (2 * WINDOW, rows), 0)
    qpos = lax.broadcasted_iota(jnp.int32, (2 * WINDOW, rows), 1) % WINDOW
    masks = []
    for off in (WINDOW, 0):
        dist = qpos + off - key
        masks.append(jnp.where((dist >= 0) & (dist < WINDOW), 0.0, NEG))
    return jnp.stack(masks).astype(jnp.float32)


def _swa(q, k, vt, sinks2):
    B, S, _ = q.shape
    n_kv = k.shape[1]
    gw = MIX_WIDTH // n_kv
    rows = (gw // HEAD_DIM) * WINDOW
    tq = SWA_Q_TILE
    return pl.pallas_call(
        _swa_kernel,
        grid=(B, n_kv, S // tq),
        in_specs=[
            pl.BlockSpec((1, tq, gw), lambda b, c, n: (b, n, c)),
            pl.BlockSpec((1, 1, S, HEAD_DIM), lambda b, c, n: (b, c, 0, 0)),
            pl.BlockSpec((1, 1, S // WINDOW, V_ROWS, WINDOW), lambda b, c, n: (b, c, 0, 0, 0)),
            pl.BlockSpec((1, 1, rows), lambda b, c, n: (c, 0, 0)),
            pl.BlockSpec((2, 2 * WINDOW, rows), lambda b, c, n: (0, 0, 0)),
        ],
        out_specs=pl.BlockSpec((1, tq, gw), lambda b, c, n: (b, n, c)),
        out_shape=jax.ShapeDtypeStruct((B, S, MIX_WIDTH), jnp.bfloat16),
        compiler_params=pltpu.CompilerParams(
            dimension_semantics=("parallel", "parallel", "arbitrary"),
            vmem_limit_bytes=VMEM_LIMIT),
        name="swa",
    )(q, k, vt, sinks2, _swa_mask(rows))


def _moba_kernel(q_ref, k_ref, vt_ref, cmask_ref, o_ref, km_ref, qs_ref, bias_ref,
                 s0_ref, s1_ref, mx0_ref, mx1_ref, m_ref, acc_ref):
    i = pl.program_id(1)
    n_kv = k_ref.shape[1]
    nb = k_ref.shape[2] // MOBA_BLOCK
    gw = q_ref.shape[2] // n_kv
    group = gw // HEAD_DIM
    rows = group * MOBA_BLOCK

    @pl.when(i == 0)
    def _():
        for c in range(n_kv):
            means = [jnp.sum(k_ref[0, c, j * MOBA_BLOCK:(j + 1) * MOBA_BLOCK, :].astype(jnp.float32),
                             axis=0, keepdims=True) for j in range(nb)]
            km = jnp.concatenate(means, axis=0) * (1.0 / MOBA_BLOCK)
            hi = km.astype(jnp.bfloat16)
            km_ref[c, 0:nb] = hi
            km_ref[c, nb:2 * nb] = (km - hi.astype(jnp.float32)).astype(jnp.bfloat16)

    def keys(c, j):
        return k_ref[0, c, pl.ds(pl.multiple_of(j * MOBA_BLOCK, MOBA_BLOCK), MOBA_BLOCK), :]

    def scores(c, j, s_ref, mx_ref):
        s = _dot_nt(keys(c, j), qs_ref[c])
        s_ref[c] = s
        mx_ref[c] = jnp.max(s, axis=0, keepdims=True)

    def probs(c, j, s_ref, mx_ref):
        bias = bias_ref[c, pl.ds(j, 1), :]
        m_old = m_ref[c]
        m_new = jnp.maximum(m_old, mx_ref[c] + bias)
        m_ref[c] = m_new
        p = jnp.exp2(s_ref[c] - (m_new - bias)).astype(jnp.bfloat16)
        return p, jnp.exp2(m_old - m_new)

    def accumulate(c, j, p, alpha):
        acc_ref[c] = acc_ref[c] * alpha + _dot(vt_ref[0, c, j], p)

    def select_blocks(c):
        gt = _dot_nt(km_ref[c], qs_ref[c])
        blk = lax.broadcasted_iota(jnp.int32, (nb, rows), 0).astype(jnp.float32)
        past = blk < i.astype(jnp.float32)
        gate = jnp.where(past, gt[0:nb] + gt[nb:2 * nb], NEG)
        chosen = jnp.zeros((nb, rows), jnp.bool_)
        for _ in range(MOBA_TOPK):
            cur = jnp.where(chosen, BELOW_NEG, gate)
            best = jnp.max(cur, axis=0, keepdims=True)
            first = jnp.min(jnp.where(cur == best, blk, float(nb)), axis=0, keepdims=True)
            chosen = chosen | (blk == first)
        bias_ref[c] = jnp.where(chosen & past, 0.0, NEG)

    ds, dm, dp = {}, {}, {}

    def own_scores(c):
        ds[c] = _dot_nt(keys(c, i), qs_ref[c]) + cmask_ref[...]
        dm[c] = jnp.max(ds[c], axis=0, keepdims=True)

    def own_probs(c):
        m_ref[c] = dm[c]
        dp[c] = jnp.exp2(ds[c] - dm[c]).astype(jnp.bfloat16)

    def own_out(c):
        acc_ref[c] = _dot(vt_ref[0, c, i], dp[c])

    for c in range(n_kv):
        qs_ref[c] = _stack_heads(q_ref[0, :, c * gw:(c + 1) * gw], group)
    for c in range(n_kv):
        scores(c, 0, s0_ref, mx0_ref)
    for slot in range(n_kv + 2):
        if slot < n_kv:
            own_scores(slot)
            select_blocks(slot)
        if 0 <= slot - 1 < n_kv:
            own_probs(slot - 1)
        if 0 <= slot - 2 < n_kv:
            own_out(slot - 2)

    def pair(u, carry):
        t = 2 * u
        nxt = jnp.minimum(t + 2, i)
        pending = None
        for par in range(2):
            cur = (s0_ref, mx0_ref) if par == 0 else (s1_ref, mx1_ref)
            ahead = (t + 1, s1_ref, mx1_ref) if par == 0 else (nxt, s0_ref, mx0_ref)
            for c in range(n_kv):
                scores(c, *ahead)
                p, alpha = probs(c, t + par, *cur)
                if pending is not None:
                    accumulate(*pending)
                pending = (c, t + par, p, alpha)
        accumulate(*pending)
        return carry

    lax.fori_loop(0, (i + 1) // 2, pair, 0)

    for c in range(n_kv):
        acc = acc_ref[c]
        ot = acc[:HEAD_DIM] / acc[HEAD_DIM:HEAD_DIM + 1]
        o_ref[0, :, c * gw:(c + 1) * gw] = _unstack_heads_t(ot, group, MOBA_BLOCK).astype(jnp.bfloat16)


def _moba_causal_mask(rows):
    key = lax.broadcasted_iota(jnp.int32, (MOBA_BLOCK, rows), 0)
    qpos = lax.broadcasted_iota(jnp.int32, (MOBA_BLOCK, rows), 1) % MOBA_BLOCK
    return jnp.where(key <= qpos, 0.0, NEG).astype(jnp.float32)


def _moba(q, k, vt):
    B, S, _ = q.shape
    n_kv = k.shape[1]
    nb = S // MOBA_BLOCK
    rows = (MIX_WIDTH // n_kv // HEAD_DIM) * MOBA_BLOCK
    return pl.pallas_call(
        _moba_kernel,
        grid=(B, nb),
        in_specs=[
            pl.BlockSpec((1, MOBA_BLOCK, MIX_WIDTH), lambda b, i: (b, i, 0)),
            pl.BlockSpec((1, n_kv, S, HEAD_DIM), lambda b, i: (b, 0, 0, 0)),
            pl.BlockSpec((1, n_kv, nb, V_ROWS, MOBA_BLOCK), lambda b, i: (b, 0, 0, 0, 0)),
            pl.BlockSpec((MOBA_BLOCK, rows), lambda b, i: (0, 0)),
        ],
        out_specs=pl.BlockSpec((1, MOBA_BLOCK, MIX_WIDTH), lambda b, i: (b, i, 0)),
        out_shape=jax.ShapeDtypeStruct((B, S, MIX_WIDTH), jnp.bfloat16),
        scratch_shapes=[
            pltpu.VMEM((n_kv, 2 * nb, HEAD_DIM), jnp.bfloat16),
            pltpu.VMEM((n_kv, rows, HEAD_DIM), jnp.bfloat16),
            pltpu.VMEM((n_kv, nb, rows), jnp.float32),
            pltpu.VMEM((n_kv, MOBA_BLOCK, rows), jnp.float32),
            pltpu.VMEM((n_kv, MOBA_BLOCK, rows), jnp.float32),
            pltpu.VMEM((n_kv, 1, rows), jnp.float32),
            pltpu.VMEM((n_kv, 1, rows), jnp.float32),
            pltpu.VMEM((n_kv, 1, rows), jnp.float32),
            pltpu.VMEM((n_kv, V_ROWS, rows), jnp.float32),
        ],
        compiler_params=pltpu.CompilerParams(
            dimension_semantics=("parallel", "arbitrary"),
            vmem_limit_bytes=VMEM_LIMIT),
        name="moba",
    )(q, k, vt, _moba_causal_mask(rows))


def _rope_tables(seq, q_scale):
    pos = jnp.arange(seq, dtype=jnp.float32)
    inv = ROPE_THETA ** (-jnp.arange(0, HEAD_DIM, 2, dtype=jnp.float32) / HEAD_DIM)
    ang = pos[:, None] * inv[None, :]
    cos, sin = jnp.cos(ang), jnp.sin(ang)
    ck = jnp.tile(cos, (1, LANES // (HEAD_DIM // 2)))
    sk = jnp.tile(jnp.concatenate([-sin, sin], axis=1), (1, LANES // HEAD_DIM))
    return ck * q_scale, sk * q_scale, ck, sk


def kernel(x, norm_g, w_in_a, sinks_a, w_in_b, w_out, final_g):
    B, S, _ = x.shape
    depth = norm_g.shape[0]
    assert S % TOKEN_TILE == 0 and S % MOBA_BLOCK == 0
    tabs = _rope_tables(S, LOG2E / math.sqrt(HEAD_DIM))

    def layer_params(i):
        g = norm_g[i].reshape(1, D_MODEL)
        if i % 2 == 0:
            return g, w_in_a[i // 2].astype(jnp.bfloat16), N_KV_A, WINDOW
        return g, w_in_b[i // 2].astype(jnp.bfloat16), N_KV_B, MOBA_BLOCK

    h = x
    q, k, vt, sz = _in_proj(h, *layer_params(0)[:2], tabs, *layer_params(0)[2:])
    for i in range(depth):
        if i % 2 == 0:
            group = N_HEADS // N_KV_A
            sinks2 = jnp.repeat(sinks_a[i // 2].astype(jnp.float32) * LOG2E, WINDOW).reshape(
                N_KV_A, 1, group * WINDOW)
            o = _swa(q, k, vt, sinks2)
        else:
            o = _moba(q, k, vt)
        w_o = w_out[i].astype(jnp.bfloat16)
        if i + 1 < depth:
            g, w, n_kv, v_blk = layer_params(i + 1)
            h, q, k, vt, sz = _mid(o, sz, h, w_o, g, w, tabs, n_kv, v_blk)
        else:
            h = _out_proj(o, sz, h, w_o, final_g.reshape(1, D_MODEL))
    return h
```

```python
import functools
import math

import jax
import jax.numpy as jnp
from jax import lax
from jax.experimental import pallas as pl
from jax.experimental.pallas import tpu as pltpu

D_MODEL = 1024
HEAD_DIM = 64
N_HEADS = 16
MIX_WIDTH = N_HEADS * HEAD_DIM
N_KV_A = 2
N_KV_B = 4
WINDOW = 128
MOBA_BLOCK = 256
MOBA_TOPK = 3
ROPE_THETA = 10000.0
NORM_EPS = 1e-5
NEG = -1e30
BELOW_NEG = -3e38
LOG2E = 1.4426950408889634

LANES = 128
MXU_COLS = 256
V_ROWS = HEAD_DIM + 16
TOKEN_TILE = 512
SWA_Q_TILE = 512
VMEM_LIMIT = 48 * 1024 * 1024

_NT = (((1,), (1,)), ((), ()))


def _dot(a, b):
    return jnp.dot(a, b, preferred_element_type=jnp.float32)


def _dot_nt(a, b):
    return lax.dot_general(a, b, _NT, preferred_element_type=jnp.float32)


def _rope(x, cos, sin_signed, first_half):
    swapped = jnp.where(first_half, pltpu.roll(x, LANES - HEAD_DIM // 2, 1),
                        pltpu.roll(x, HEAD_DIM // 2, 1))
    return x * cos + swapped * sin_signed


def _project(x, g_ref, w_ref, tab_refs, q_ref, k_ref, vt_ref, z_ref, n_kv, v_blk):
    tm = x.shape[0]
    kvw = n_kv * HEAD_DIM
    ms = jnp.mean(x * x, axis=-1, keepdims=True)
    hn = (x * lax.rsqrt(ms + NORM_EPS) * g_ref[...]).astype(jnp.bfloat16)

    lane = lax.broadcasted_iota(jnp.int32, (tm, LANES), 1)
    first_half = (lane % HEAD_DIM) < (HEAD_DIM // 2)
    cq, sq, ck, sk = (t[...] for t in tab_refs)

    for c in range(MIX_WIDTH // MXU_COLS):
        q = _dot(hn, w_ref[:, c * MXU_COLS:(c + 1) * MXU_COLS])
        for hh in range(MXU_COLS // LANES):
            lo = c * MXU_COLS + hh * LANES
            q_ref[0, :, lo:lo + LANES] = _rope(
                q[:, hh * LANES:(hh + 1) * LANES], cq, sq, first_half).astype(jnp.bfloat16)

    kv = _dot(hn, w_ref[:, MIX_WIDTH:MIX_WIDTH + 2 * kvw])
    for c in range(kvw // LANES):
        k = _rope(kv[:, c * LANES:(c + 1) * LANES], ck, sk, first_half).astype(jnp.bfloat16)
        k_ref[0, 2 * c] = k[:, :HEAD_DIM]
        k_ref[0, 2 * c + 1] = k[:, HEAD_DIM:]

    ones = jnp.ones((V_ROWS - HEAD_DIM, v_blk), jnp.bfloat16)
    for c in range(kvw // LANES):
        vt = kv[:, kvw + c * LANES:kvw + (c + 1) * LANES].T.astype(jnp.bfloat16)
        for hh in range(2):
            for b in range(tm // v_blk):
                vt_ref[0, 2 * c + hh, b, 0:HEAD_DIM, :] = vt[hh * HEAD_DIM:(hh + 1) * HEAD_DIM,
                                                              b * v_blk:(b + 1) * v_blk]
                vt_ref[0, 2 * c + hh, b, HEAD_DIM:V_ROWS, :] = ones

    z0 = MIX_WIDTH + 2 * kvw
    for c in range(MIX_WIDTH // MXU_COLS):
        z = _dot(hn, w_ref[:, z0 + c * MXU_COLS:z0 + (c + 1) * MXU_COLS])
        z_ref[0, :, c * MXU_COLS:(c + 1) * MXU_COLS] = (
            z * (1.0 / (1.0 + jnp.exp(-z)))).astype(jnp.bfloat16)


def _in_proj_kernel(n_kv, v_blk, h_ref, g_ref, w_ref, cq_ref, sq_ref, ck_ref, sk_ref,
                    q_ref, k_ref, vt_ref, z_ref):
    _project(h_ref[0], g_ref, w_ref, (cq_ref, sq_ref, ck_ref, sk_ref),
             q_ref, k_ref, vt_ref, z_ref, n_kv, v_blk)


def _mid_kernel(n_kv, v_blk, o_ref, sz_ref, h_ref, wo_ref, g_ref, w_ref,
                cq_ref, sq_ref, ck_ref, sk_ref, hout_ref, q_ref, k_ref, vt_ref, z_ref):
    h = h_ref[0] + _dot(o_ref[0] * sz_ref[0], wo_ref[...])
    hout_ref[0] = h
    _project(h, g_ref, w_ref, (cq_ref, sq_ref, ck_ref, sk_ref),
             q_ref, k_ref, vt_ref, z_ref, n_kv, v_blk)


def _proj_out_specs(B, S, tm, n_kv, v_blk):
    specs = [
        pl.BlockSpec((1, tm, MIX_WIDTH), lambda b, t: (b, t, 0)),
        pl.BlockSpec((1, n_kv, tm, HEAD_DIM), lambda b, t: (b, 0, t, 0)),
        pl.BlockSpec((1, n_kv, tm // v_blk, V_ROWS, v_blk), lambda b, t: (b, 0, t, 0, 0)),
        pl.BlockSpec((1, tm, MIX_WIDTH), lambda b, t: (b, t, 0)),
    ]
    shapes = [
        jax.ShapeDtypeStruct((B, S, MIX_WIDTH), jnp.bfloat16),
        jax.ShapeDtypeStruct((B, n_kv, S, HEAD_DIM), jnp.bfloat16),
        jax.ShapeDtypeStruct((B, n_kv, S // v_blk, V_ROWS, v_blk), jnp.bfloat16),
        jax.ShapeDtypeStruct((B, S, MIX_WIDTH), jnp.bfloat16),
    ]
    return specs, shapes


def _resident(shape):
    return pl.BlockSpec(shape, lambda b, t: (0,) * len(shape), pipeline_mode=pl.Buffered(1))


def _in_proj(h, g, w, tabs, n_kv, v_blk):
    B, S, _ = h.shape
    tm = TOKEN_TILE
    tab_spec = pl.BlockSpec((tm, LANES), lambda b, t: (t, 0))
    out_specs, out_shape = _proj_out_specs(B, S, tm, n_kv, v_blk)
    return pl.pallas_call(
        functools.partial(_in_proj_kernel, n_kv, v_blk),
        grid=(B, S // tm),
        in_specs=[
            pl.BlockSpec((1, tm, D_MODEL), lambda b, t: (b, t, 0)),
            _resident((1, D_MODEL)),
            _resident(w.shape),
            tab_spec, tab_spec, tab_spec, tab_spec,
        ],
        out_specs=out_specs,
        out_shape=out_shape,
        compiler_params=pltpu.CompilerParams(
            dimension_semantics=("parallel", "parallel"), vmem_limit_bytes=VMEM_LIMIT),
        name="in_proj",
    )(h, g, w, *tabs)


def _mid(o, sz, h, w_o, g, w, tabs, n_kv, v_blk):
    B, S, _ = h.shape
    tm = TOKEN_TILE
    row_spec = pl.BlockSpec((1, tm, D_MODEL), lambda b, t: (b, t, 0))
    tab_spec = pl.BlockSpec((tm, LANES), lambda b, t: (t, 0))
    out_specs, out_shape = _proj_out_specs(B, S, tm, n_kv, v_blk)
    return pl.pallas_call(
        functools.partial(_mid_kernel, n_kv, v_blk),
        grid=(B, S // tm),
        in_specs=[
            row_spec, row_spec, row_spec,
            _resident(w_o.shape),
            _resident((1, D_MODEL)),
            _resident(w.shape),
            tab_spec, tab_spec, tab_spec, tab_spec,
        ],
        out_specs=[row_spec] + out_specs,
        out_shape=[jax.ShapeDtypeStruct((B, S, D_MODEL), jnp.float32)] + out_shape,
        compiler_params=pltpu.CompilerParams(
            dimension_semantics=("parallel", "parallel"), vmem_limit_bytes=VMEM_LIMIT),
        name="mid",
    )(o, sz, h, w_o, g, w, *tabs)


def _out_proj_kernel(o_ref, z_ref, h_ref, w_ref, g_ref, out_ref):
    h = h_ref[0] + _dot(o_ref[0] * z_ref[0], w_ref[...])
    ms = jnp.mean(h * h, axis=-1, keepdims=True)
    out_ref[0] = h * lax.rsqrt(ms + NORM_EPS) * g_ref[...]


def _out_proj(o, sz, h, w, g):
    B, S, _ = h.shape
    tm = TOKEN_TILE
    row_spec = pl.BlockSpec((1, tm, D_MODEL), lambda b, t: (b, t, 0))
    return pl.pallas_call(
        _out_proj_kernel,
        grid=(B, S // tm),
        in_specs=[row_spec, row_spec, row_spec, _resident(w.shape), _resident((1, D_MODEL))],
        out_specs=row_spec,
        out_shape=jax.ShapeDtypeStruct((B, S, D_MODEL), jnp.float32),
        compiler_params=pltpu.CompilerParams(
            dimension_semantics=("parallel", "parallel"), vmem_limit_bytes=VMEM_LIMIT),
        name="out_proj",
    )(o, sz, h, w, g)


def _stack_heads(q, n):
    return jnp.concatenate([q[:, g * HEAD_DIM:(g + 1) * HEAD_DIM] for g in range(n)], axis=0)


def _unstack_heads_t(ot, n, rows):
    st = jnp.concatenate([ot[:, g * rows:(g + 1) * rows] for g in range(n)], axis=0)
    return st.T


def _swa_kernel(q_ref, k_ref, vt_ref, sink_ref, mask_ref, o_ref, s_ref, p_ref):
    n = pl.program_id(1)
    n_kv = k_ref.shape[1]
    gw = q_ref.shape[2] // n_kv
    group = gw // HEAD_DIM
    n_sub = q_ref.shape[1] // WINDOW
    units = [(c, sb) for sb in range(n_sub) for c in range(n_kv)]
    mx, m = {}, {}

    def band(sb):
        qb = n * n_sub + sb
        return qb, jnp.maximum(qb - 1, 0)

    def scores(u):
        c, sb = units[u]
        qb, first = band(sb)
        qs = _stack_heads(q_ref[0, sb * WINDOW:(sb + 1) * WINDOW, c * gw:(c + 1) * gw], group)
        kb = k_ref[0, c, pl.ds(pl.multiple_of(first * WINDOW, WINDOW), 2 * WINDOW), :]
        st = _dot_nt(kb, qs) + mask_ref[jnp.where(qb == 0, 1, 0)]
        s_ref[u % 3] = st
        mx[u] = jnp.max(st, axis=0, keepdims=True)

    def probs(u):
        c, _ = units[u]
        m[u] = jnp.maximum(mx[u], sink_ref[c])
        p_ref[u % 2] = jnp.exp2(s_ref[u % 3 + jnp.minimum(n, 0)] - m[u]).astype(jnp.bfloat16)

    def output(u):
        c, sb = units[u]
        _, first = band(sb)
        vt = jnp.concatenate([vt_ref[0, c, first], vt_ref[0, c, first + 1]], axis=1)
        acc = _dot(vt, p_ref[u % 2])
        denom = acc[HEAD_DIM:HEAD_DIM + 1] + jnp.exp2(sink_ref[c] - m[u])
        ot = acc[:HEAD_DIM] / denom
        o_ref[0, sb * WINDOW:(sb + 1) * WINDOW, c * gw:(c + 1) * gw] = _unstack_heads_t(
            ot, group, WINDOW).astype(jnp.bfloat16)

    for slot in range(len(units) + 2):
        if slot < len(units):
            scores(slot)
        if 0 <= slot - 1 < len(units):
            probs(slot - 1)
        if 0 <= slot - 2 < len(units):
            output(slot - 2)


def _swa_mask(rows):
    key = lax.broadcasted_iota(jnp.int32, (2 * WINDOW, rows), 0)
    qpos = lax.broadcasted_iota(jnp.int32, (2 * WINDOW, rows), 1) % WINDOW
    masks = []
    for off in (WINDOW, 0):
        dist = qpos + off - key
        masks.append(jnp.where((dist >= 0) & (dist < WINDOW), 0.0, NEG))
    return jnp.stack(masks).astype(jnp.float32)


def _swa(q, k, vt, sinks2):
    B, S, _ = q.shape
    n_kv = k.shape[1]
    rows = (MIX_WIDTH // n_kv // HEAD_DIM) * WINDOW
    tq = SWA_Q_TILE
    return pl.pallas_call(
        _swa_kernel,
        grid=(B, S // tq),
        in_specs=[
            pl.BlockSpec((1, tq, MIX_WIDTH), lambda b, n: (b, n, 0)),
            pl.BlockSpec((1, n_kv, S, HEAD_DIM), lambda b, n: (b, 0, 0, 0)),
            pl.BlockSpec((1, n_kv, S // WINDOW, V_ROWS, WINDOW), lambda b, n: (b, 0, 0, 0, 0)),
            pl.BlockSpec((n_kv, 1, rows), lambda b, n: (0, 0, 0)),
            pl.BlockSpec((2, 2 * WINDOW, rows), lambda b, n: (0, 0, 0)),
        ],
        out_specs=pl.BlockSpec((1, tq, MIX_WIDTH), lambda b, n: (b, n, 0)),
        out_shape=jax.ShapeDtypeStruct((B, S, MIX_WIDTH), jnp.bfloat16),
        scratch_shapes=[
            pltpu.VMEM((3, 2 * WINDOW, rows), jnp.float32),
            pltpu.VMEM((2, 2 * WINDOW, rows), jnp.bfloat16),
        ],
        compiler_params=pltpu.CompilerParams(
            dimension_semantics=("parallel", "arbitrary"),
            vmem_limit_bytes=VMEM_LIMIT),
        name="swa",
    )(q, k, vt, sinks2, _swa_mask(rows))


def _moba_kernel(q_ref, k_ref, vt_ref, cmask_ref, o_ref, km_ref, qs_ref, bias_ref,
                 s0_ref, s1_ref, mx0_ref, mx1_ref, p0_ref, p1_ref, m_ref, acc_ref):
    i = pl.program_id(1)
    n_kv = k_ref.shape[1]
    nb = k_ref.shape[2] // MOBA_BLOCK
    gw = q_ref.shape[2] // n_kv
    group = gw // HEAD_DIM
    rows = group * MOBA_BLOCK

    @pl.when(i == 0)
    def _():
        for c in range(n_kv):
            means = [jnp.sum(k_ref[0, c, j * MOBA_BLOCK:(j + 1) * MOBA_BLOCK, :].astype(jnp.float32),
                             axis=0, keepdims=True) for j in range(nb)]
            km = jnp.concatenate(means, axis=0) * (1.0 / MOBA_BLOCK)
            hi = km.astype(jnp.bfloat16)
            km_ref[c, 0:nb] = hi
            km_ref[c, nb:2 * nb] = (km - hi.astype(jnp.float32)).astype(jnp.bfloat16)

    def keys(c, j):
        return k_ref[0, c, pl.ds(pl.multiple_of(j * MOBA_BLOCK, MOBA_BLOCK), MOBA_BLOCK), :]

    def scores(c, j, s_ref, mx_ref):
        kj = keys(c, j)
        for g in range(group):
            cols = slice(g * MOBA_BLOCK, (g + 1) * MOBA_BLOCK)
            s = _dot_nt(kj, qs_ref[c, cols, :])
            s_ref[c, :, cols] = s
            mx_ref[c, :, cols] = jnp.max(s, axis=0, keepdims=True)

    def probs(c, j, s_ref, mx_ref, p_ref):
        bias = bias_ref[c, pl.ds(j, 1), :]
        m_old = m_ref[c]
        m_new = jnp.maximum(m_old, mx_ref[c] + bias)
        m_ref[c] = m_new
        shift = m_new - bias
        for g in range(group):
            cols = slice(g * MOBA_BLOCK, (g + 1) * MOBA_BLOCK)
            p_ref[:, cols] = jnp.exp2(s_ref[c, :, cols] - shift[:, cols]).astype(jnp.bfloat16)
        return jnp.exp2(m_old - m_new)

    def accumulate(c, j, p_ref, alpha):
        acc_ref[c] = acc_ref[c] * alpha + _dot(vt_ref[0, c, j], p_ref[...])

    def select_blocks(c):
        gt = _dot_nt(km_ref[c], qs_ref[c])
        blk = lax.broadcasted_iota(jnp.int32, (nb, rows), 0).astype(jnp.float32)
        past = blk < i.astype(jnp.float32)
        gate = jnp.where(past, gt[0:nb] + gt[nb:2 * nb], NEG)
        chosen = jnp.zeros((nb, rows), jnp.bool_)
        for _ in range(MOBA_TOPK):
            cur = jnp.where(chosen, BELOW_NEG, gate)
            best = jnp.max(cur, axis=0, keepdims=True)
            first = jnp.min(jnp.where(cur == best, blk, float(nb)), axis=0, keepdims=True)
            chosen = chosen | (blk == first)
        bias_ref[c] = jnp.where(chosen & past, 0.0, NEG)

    p_refs = (p0_ref, p1_ref)
    ds, dm, dp = {}, {}, {}

    def own_scores(c):
        ds[c] = _dot_nt(keys(c, i), qs_ref[c]) + cmask_ref[...]
        dm[c] = jnp.max(ds[c], axis=0, keepdims=True)

    def own_probs(c):
        m_ref[c] = dm[c]
        dp[c] = jnp.exp2(ds[c] - dm[c]).astype(jnp.bfloat16)

    def own_out(c):
        acc_ref[c] = _dot(vt_ref[0, c, i], dp[c])

    for c in range(n_kv):
        qs_ref[c] = _stack_heads(q_ref[0, :, c * gw:(c + 1) * gw], group)
    bufs = ((s0_ref, mx0_ref), (s1_ref, mx1_ref))
    for slot in range(n_kv + 2):
        if slot < n_kv:
            own_scores(slot)
            select_blocks(slot)
            scores(slot, 0, *bufs[0])
            scores(slot, 1, *bufs[1])
        if 0 <= slot - 1 < n_kv:
            own_probs(slot - 1)
        if 0 <= slot - 2 < n_kv:
            own_out(slot - 2)

    def pair(t, refill):
        pending = None
        for par in range(2):
            for c in range(n_kv):
                p_ref = p_refs[c % 2]
                if pending is not None and refill:
                    scores(pending[0], pending[4], *pending[5])
                if pending is not None:
                    accumulate(*pending[:4])
                alpha = probs(c, t + par, *bufs[par], p_ref)
                pending = (c, t + par, p_ref, alpha, t + 2 + par, bufs[par])
        accumulate(*pending[:4])
        if refill:
            scores(pending[0], pending[4], *pending[5])

    n_pairs = (i + 1) // 2

    def steady(u, carry):
        pair(2 * u, True)
        return carry

    lax.fori_loop(0, n_pairs - 1, steady, 0)

    @pl.when(n_pairs > 0)
    def _():
        pair(2 * (n_pairs - 1), False)

    for c in range(n_kv):
        acc = acc_ref[c]
        ot = acc[:HEAD_DIM] / acc[HEAD_DIM:HEAD_DIM + 1]
        o_ref[0, :, c * gw:(c + 1) * gw] = _unstack_heads_t(ot, group, MOBA_BLOCK).astype(jnp.bfloat16)


def _moba_causal_mask(rows):
    key = lax.broadcasted_iota(jnp.int32, (MOBA_BLOCK, rows), 0)
    qpos = lax.broadcasted_iota(jnp.int32, (MOBA_BLOCK, rows), 1) % MOBA_BLOCK
    return jnp.where(key <= qpos, 0.0, NEG).astype(jnp.float32)


def _moba(q, k, vt):
    B, S, _ = q.shape
    n_kv = k.shape[1]
    nb = S // MOBA_BLOCK
    rows = (MIX_WIDTH // n_kv // HEAD_DIM) * MOBA_BLOCK
    return pl.pallas_call(
        _moba_kernel,
        grid=(B, nb),
        in_specs=[
            pl.BlockSpec((1, MOBA_BLOCK, MIX_WIDTH), lambda b, i: (b, i, 0)),
            pl.BlockSpec((1, n_kv, S, HEAD_DIM), lambda b, i: (b, 0, 0, 0)),
            pl.BlockSpec((1, n_kv, nb, V_ROWS, MOBA_BLOCK), lambda b, i: (b, 0, 0, 0, 0)),
            pl.BlockSpec((MOBA_BLOCK, rows), lambda b, i: (0, 0)),
        ],
        out_specs=pl.BlockSpec((1, MOBA_BLOCK, MIX_WIDTH), lambda b, i: (b, i, 0)),
        out_shape=jax.ShapeDtypeStruct((B, S, MIX_WIDTH), jnp.bfloat16),
        scratch_shapes=[
            pltpu.VMEM((n_kv, 2 * nb, HEAD_DIM), jnp.bfloat16),
            pltpu.VMEM((n_kv, rows, HEAD_DIM), jnp.bfloat16),
            pltpu.VMEM((n_kv, nb, rows), jnp.float32),
            pltpu.VMEM((n_kv, MOBA_BLOCK, rows), jnp.float32),
            pltpu.VMEM((n_kv, MOBA_BLOCK, rows), jnp.float32),
            pltpu.VMEM((n_kv, 1, rows), jnp.float32),
            pltpu.VMEM((n_kv, 1, rows), jnp.float32),
            pltpu.VMEM((MOBA_BLOCK, rows), jnp.bfloat16),
            pltpu.VMEM((MOBA_BLOCK, rows), jnp.bfloat16),
            pltpu.VMEM((n_kv, 1, rows), jnp.float32),
            pltpu.VMEM((n_kv, V_ROWS, rows), jnp.float32),
        ],
        compiler_params=pltpu.CompilerParams(
            dimension_semantics=("parallel", "arbitrary"),
            vmem_limit_bytes=VMEM_LIMIT),
        name="moba",
    )(q, k, vt, _moba_causal_mask(rows))


def _rope_tables(seq, q_scale):
    pos = jnp.arange(seq, dtype=jnp.float32)
    inv = ROPE_THETA ** (-jnp.arange(0, HEAD_DIM, 2, dtype=jnp.float32) / HEAD_DIM)
    ang = pos[:, None] * inv[None, :]
    cos, sin = jnp.cos(ang), jnp.sin(ang)
    ck = jnp.tile(cos, (1, LANES // (HEAD_DIM // 2)))
    sk = jnp.tile(jnp.concatenate([-sin, sin], axis=1), (1, LANES // HEAD_DIM))
    return ck * q_scale, sk * q_scale, ck, sk


def kernel(x, norm_g, w_in_a, sinks_a, w_in_b, w_out, final_g):
    B, S, _ = x.shape
    depth = norm_g.shape[0]
    assert S % TOKEN_TILE == 0 and S % MOBA_BLOCK == 0
    tabs = _rope_tables(S, LOG2E / math.sqrt(HEAD_DIM))

    def layer_params(i):
        g = norm_g[i].reshape(1, D_MODEL)
        if i % 2 == 0:
            return g, w_in_a[i // 2].astype(jnp.bfloat16), N_KV_A, WINDOW
        return g, w_in_b[i // 2].astype(jnp.bfloat16), N_KV_B, MOBA_BLOCK

    h = x
    q, k, vt, sz = _in_proj(h, *layer_params(0)[:2], tabs, *layer_params(0)[2:])
    for i in range(depth):
        if i % 2 == 0:
            group = N_HEADS // N_KV_A
            sinks2 = jnp.repeat(sinks_a[i // 2].astype(jnp.float32) * LOG2E, WINDOW).reshape(
                N_KV_A, 1, group * WINDOW)
            o = _swa(q, k, vt, sinks2)
        else:
            o = _moba(q, k, vt)
        w_o = w_out[i].astype(jnp.bfloat16)
        if i + 1 < depth:
            g, w, n_kv, v_blk = layer_params(i + 1)
            h, q, k, vt, sz = _mid(o, sz, h, w_o, g, w, tabs, n_kv, v_blk)
        else:
            h = _out_proj(o, sz, h, w_o, final_g.reshape(1, D_MODEL))
    return h
```

```python
import functools
import math

import jax
import jax.numpy as jnp
from jax import lax
from jax.experimental import pallas as pl
from jax.experimental.pallas import tpu as pltpu

D_MODEL = 1024
HEAD_DIM = 64
N_HEADS = 16
MIX_WIDTH = N_HEADS * HEAD_DIM
N_KV_A = 2
N_KV_B = 4
WINDOW = 128
MOBA_BLOCK = 256
MOBA_TOPK = 3
ROPE_THETA = 10000.0
NORM_EPS = 1e-5
NEG = -1e30
BELOW_NEG = -3e38
LOG2E = 1.4426950408889634

LANES = 128
MXU_COLS = 256
V_ROWS = HEAD_DIM + 16
TOKEN_TILE = 512
SWA_Q_TILE = 1024
VMEM_LIMIT = 48 * 1024 * 1024

_NT = (((1,), (1,)), ((), ()))


def _dot(a, b):
    return jnp.dot(a, b, preferred_element_type=jnp.float32)


def _dot_nt(a, b):
    return lax.dot_general(a, b, _NT, preferred_element_type=jnp.float32)


def _rope(x, cos, sin_signed, first_half):
    swapped = jnp.where(first_half, pltpu.roll(x, LANES - HEAD_DIM // 2, 1),
                        pltpu.roll(x, HEAD_DIM // 2, 1))
    return x * cos + swapped * sin_signed


def _project(x, g_ref, w_ref, tab_refs, q_ref, k_ref, vt_ref, z_ref, n_kv, v_blk):
    tm = x.shape[0]
    kvw = n_kv * HEAD_DIM
    ms = jnp.mean(x * x, axis=-1, keepdims=True)
    hn = (x * lax.rsqrt(ms + NORM_EPS) * g_ref[...]).astype(jnp.bfloat16)

    lane = lax.broadcasted_iota(jnp.int32, (tm, LANES), 1)
    first_half = (lane % HEAD_DIM) < (HEAD_DIM // 2)
    cq, sq, ck, sk = (t[...] for t in tab_refs)

    for c in range(MIX_WIDTH // MXU_COLS):
        q = _dot(hn, w_ref[:, c * MXU_COLS:(c + 1) * MXU_COLS])
        for hh in range(MXU_COLS // LANES):
            lo = c * MXU_COLS + hh * LANES
            q_ref[0, :, lo:lo + LANES] = _rope(
                q[:, hh * LANES:(hh + 1) * LANES], cq, sq, first_half).astype(jnp.bfloat16)

    kv = _dot(hn, w_ref[:, MIX_WIDTH:MIX_WIDTH + 2 * kvw])
    for c in range(kvw // LANES):
        k = _rope(kv[:, c * LANES:(c + 1) * LANES], ck, sk, first_half).astype(jnp.bfloat16)
        k_ref[0, 2 * c] = k[:, :HEAD_DIM]
        k_ref[0, 2 * c + 1] = k[:, HEAD_DIM:]

    ones = jnp.ones((V_ROWS - HEAD_DIM, v_blk), jnp.bfloat16)
    for c in range(kvw // LANES):
        vt = kv[:, kvw + c * LANES:kvw + (c + 1) * LANES].T.astype(jnp.bfloat16)
        for hh in range(2):
            for b in range(tm // v_blk):
                vt_ref[0, 2 * c + hh, b, 0:HEAD_DIM, :] = vt[hh * HEAD_DIM:(hh + 1) * HEAD_DIM,
                                                              b * v_blk:(b + 1) * v_blk]
                vt_ref[0, 2 * c + hh, b, HEAD_DIM:V_ROWS, :] = ones

    z0 = MIX_WIDTH + 2 * kvw
    for c in range(MIX_WIDTH // MXU_COLS):
        z = _dot(hn, w_ref[:, z0 + c * MXU_COLS:z0 + (c + 1) * MXU_COLS])
        z_ref[0, :, c * MXU_COLS:(c + 1) * MXU_COLS] = (
            z * (1.0 / (1.0 + jnp.exp(-z)))).astype(jnp.bfloat16)


def _in_proj_kernel(n_kv, v_blk, h_ref, g_ref, w_ref, cq_ref, sq_ref, ck_ref, sk_ref,
                    q_ref, k_ref, vt_ref, z_ref):
    _project(h_ref[0], g_ref, w_ref, (cq_ref, sq_ref, ck_ref, sk_ref),
             q_ref, k_ref, vt_ref, z_ref, n_kv, v_blk)


def _mid_kernel(n_kv, v_blk, o_ref, sz_ref, h_ref, wo_ref, g_ref, w_ref,
                cq_ref, sq_ref, ck_ref, sk_ref, hout_ref, q_ref, k_ref, vt_ref, z_ref):
    h = h_ref[0] + _dot(o_ref[0] * sz_ref[0], wo_ref[...])
    hout_ref[0] = h
    _project(h, g_ref, w_ref, (cq_ref, sq_ref, ck_ref, sk_ref),
             q_ref, k_ref, vt_ref, z_ref, n_kv, v_blk)


def _proj_out_specs(B, S, tm, n_kv, v_blk):
    specs = [
        pl.BlockSpec((1, tm, MIX_WIDTH), lambda b, t: (b, t, 0)),
        pl.BlockSpec((1, n_kv, tm, HEAD_DIM), lambda b, t: (b, 0, t, 0)),
        pl.BlockSpec((1, n_kv, tm // v_blk, V_ROWS, v_blk), lambda b, t: (b, 0, t, 0, 0)),
        pl.BlockSpec((1, tm, MIX_WIDTH), lambda b, t: (b, t, 0)),
    ]
    shapes = [
        jax.ShapeDtypeStruct((B, S, MIX_WIDTH), jnp.bfloat16),
        jax.ShapeDtypeStruct((B, n_kv, S, HEAD_DIM), jnp.bfloat16),
        jax.ShapeDtypeStruct((B, n_kv, S // v_blk, V_ROWS, v_blk), jnp.bfloat16),
        jax.ShapeDtypeStruct((B, S, MIX_WIDTH), jnp.bfloat16),
    ]
    return specs, shapes


def _resident(shape):
    return pl.BlockSpec(shape, lambda b, t: (0,) * len(shape), pipeline_mode=pl.Buffered(1))


def _in_proj(h, g, w, tabs, n_kv, v_blk):
    B, S, _ = h.shape
    tm = TOKEN_TILE
    tab_spec = pl.BlockSpec((tm, LANES), lambda b, t: (t, 0))
    out_specs, out_shape = _proj_out_specs(B, S, tm, n_kv, v_blk)
    return pl.pallas_call(
        functools.partial(_in_proj_kernel, n_kv, v_blk),
        grid=(B, S // tm),
        in_specs=[
            pl.BlockSpec((1, tm, D_MODEL), lambda b, t: (b, t, 0)),
            _resident((1, D_MODEL)),
            _resident(w.shape),
            tab_spec, tab_spec, tab_spec, tab_spec,
        ],
        out_specs=out_specs,
        out_shape=out_shape,
        compiler_params=pltpu.CompilerParams(
            dimension_semantics=("parallel", "parallel"), vmem_limit_bytes=VMEM_LIMIT),
        name="in_proj",
    )(h, g, w, *tabs)


def _mid(o, sz, h, w_o, g, w, tabs, n_kv, v_blk):
    B, S, _ = h.shape
    tm = TOKEN_TILE
    row_spec = pl.BlockSpec((1, tm, D_MODEL), lambda b, t: (b, t, 0))
    tab_spec = pl.BlockSpec((tm, LANES), lambda b, t: (t, 0))
    out_specs, out_shape = _proj_out_specs(B, S, tm, n_kv, v_blk)
    return pl.pallas_call(
        functools.partial(_mid_kernel, n_kv, v_blk),
        grid=(B, S // tm),
        in_specs=[
            row_spec, row_spec, row_spec,
            _resident(w_o.shape),
            _resident((1, D_MODEL)),
            _resident(w.shape),
            tab_spec, tab_spec, tab_spec, tab_spec,
        ],
        out_specs=[row_spec] + out_specs,
        out_shape=[jax.ShapeDtypeStruct((B, S, D_MODEL), jnp.float32)] + out_shape,
        compiler_params=pltpu.CompilerParams(
            dimension_semantics=("parallel", "parallel"), vmem_limit_bytes=VMEM_LIMIT),
        name="mid",
    )(o, sz, h, w_o, g, w, *tabs)


def _out_proj_kernel(o_ref, z_ref, h_ref, w_ref, g_ref, out_ref):
    h = h_ref[0] + _dot(o_ref[0] * z_ref[0], w_ref[...])
    ms = jnp.mean(h * h, axis=-1, keepdims=True)
    out_ref[0] = h * lax.rsqrt(ms + NORM_EPS) * g_ref[...]


def _out_proj(o, sz, h, w, g):
    B, S, _ = h.shape
    tm = TOKEN_TILE
    row_spec = pl.BlockSpec((1, tm, D_MODEL), lambda b, t: (b, t, 0))
    return pl.pallas_call(
        _out_proj_kernel,
        grid=(B, S // tm),
        in_specs=[row_spec, row_spec, row_spec, _resident(w.shape), _resident((1, D_MODEL))],
        out_specs=row_spec,
        out_shape=jax.ShapeDtypeStruct((B, S, D_MODEL), jnp.float32),
        compiler_params=pltpu.CompilerParams(
            dimension_semantics=("parallel", "parallel"), vmem_limit_bytes=VMEM_LIMIT),
        name="out_proj",
    )(o, sz, h, w, g)


def _stack_heads(q, n):
    return jnp.concatenate([q[:, g * HEAD_DIM:(g + 1) * HEAD_DIM] for g in range(n)], axis=0)


def _unstack_heads_t(ot, n, rows):
    st = jnp.concatenate([ot[:, g * rows:(g + 1) * rows] for g in range(n)], axis=0)
    return st.T


def _swa_kernel(q_ref, k_ref, vt_ref, sink_ref, mask_ref, o_ref, s_ref, p_ref):
    n = pl.program_id(1)
    n_kv = k_ref.shape[1]
    gw = q_ref.shape[2] // n_kv
    group = gw // HEAD_DIM
    n_sub = q_ref.shape[1] // WINDOW
    units = [(c, sb) for sb in range(n_sub) for c in range(n_kv)]
    mx, m = {}, {}

    def band(sb):
        qb = n * n_sub + sb
        return qb, jnp.maximum(qb - 1, 0)

    def scores(u):
        c, sb = units[u]
        qb, first = band(sb)
        qs = _stack_heads(q_ref[0, sb * WINDOW:(sb + 1) * WINDOW, c * gw:(c + 1) * gw], group)
        kb = k_ref[0, c, pl.ds(pl.multiple_of(first * WINDOW, WINDOW), 2 * WINDOW), :]
        st = _dot_nt(kb, qs) + mask_ref[jnp.where(qb == 0, 1, 0)]
        s_ref[u % 3] = st
        mx[u] = jnp.max(st, axis=0, keepdims=True)

    def probs(u):
        c, _ = units[u]
        m[u] = jnp.maximum(mx[u], sink_ref[c])
        p_ref[u % 2] = jnp.exp2(s_ref[u % 3 + jnp.minimum(n, 0)] - m[u]).astype(jnp.bfloat16)

    def output(u):
        c, sb = units[u]
        _, first = band(sb)
        vt = jnp.concatenate([vt_ref[0, c, first], vt_ref[0, c, first + 1]], axis=1)
        acc = _dot(vt, p_ref[u % 2])
        denom = acc[HEAD_DIM:HEAD_DIM + 1] + jnp.exp2(sink_ref[c] - m[u])
        ot = acc[:HEAD_DIM] / denom
        o_ref[0, sb * WINDOW:(sb + 1) * WINDOW, c * gw:(c + 1) * gw] = _unstack_heads_t(
            ot, group, WINDOW).astype(jnp.bfloat16)

    for slot in range(len(units) + 2):
        if slot < len(units):
            scores(slot)
        if 0 <= slot - 1 < len(units):
            probs(slot - 1)
        if 0 <= slot - 2 < len(units):
            output(slot - 2)


def _swa_mask(rows):
    key = lax.broadcasted_iota(jnp.int32, (2 * WINDOW, rows), 0)
    qpos = lax.broadcasted_iota(jnp.int32, (2 * WINDOW, rows), 1) % WINDOW
    masks = []
    for off in (WINDOW, 0):
        dist = qpos + off - key
        masks.append(jnp.where((dist >= 0) & (dist < WINDOW), 0.0, NEG))
    return jnp.stack(masks).astype(jnp.float32)


def _swa(q, k, vt, sinks2):
    B, S, _ = q.shape
    n_kv = k.shape[1]
    rows = (MIX_WIDTH // n_kv // HEAD_DIM) * WINDOW
    tq = SWA_Q_TILE
    return pl.pallas_call(
        _swa_kernel,
        grid=(B, S // tq),
        in_specs=[
            pl.BlockSpec((1, tq, MIX_WIDTH), lambda b, n: (b, n, 0)),
            pl.BlockSpec((1, n_kv, S, HEAD_DIM), lambda b, n: (b, 0, 0, 0)),
            pl.BlockSpec((1, n_kv, S // WINDOW, V_ROWS, WINDOW), lambda b, n: (b, 0, 0, 0, 0)),
            pl.BlockSpec((n_kv, 1, rows), lambda b, n: (0, 0, 0)),
            pl.BlockSpec((2, 2 * WINDOW, rows), lambda b, n: (0, 0, 0)),
        ],
        out_specs=pl.BlockSpec((1, tq, MIX_WIDTH), lambda b, n: (b, n, 0)),
        out_shape=jax.ShapeDtypeStruct((B, S, MIX_WIDTH), jnp.bfloat16),
        scratch_shapes=[
            pltpu.VMEM((3, 2 * WINDOW, rows), jnp.float32),
            pltpu.VMEM((2, 2 * WINDOW, rows), jnp.bfloat16),
        ],
        compiler_params=pltpu.CompilerParams(
            dimension_semantics=("parallel", "arbitrary"),
            vmem_limit_bytes=VMEM_LIMIT),
        name="swa",
    )(q, k, vt, sinks2, _swa_mask(rows))


def _moba_kernel(q_ref, k_ref, vt_ref, cmask_ref, o_ref, km_ref, qs_ref, bias_ref,
                 s0_ref, s1_ref, mx0_ref, mx1_ref, p0_ref, p1_ref, m_ref, acc_ref):
    i = pl.program_id(1)
    n_kv = k_ref.shape[1]
    nb = k_ref.shape[2] // MOBA_BLOCK
    gw = q_ref.shape[2] // n_kv
    group = gw // HEAD_DIM
    rows = group * MOBA_BLOCK

    @pl.when(i == 0)
    def _():
        for c in range(n_kv):
            means = [jnp.sum(k_ref[0, c, j * MOBA_BLOCK:(j + 1) * MOBA_BLOCK, :].astype(jnp.float32),
                             axis=0, keepdims=True) for j in range(nb)]
            km = jnp.concatenate(means, axis=0) * (1.0 / MOBA_BLOCK)
            hi = km.astype(jnp.bfloat16)
            km_ref[c, 0:nb] = hi
            km_ref[c, nb:2 * nb] = (km - hi.astype(jnp.float32)).astype(jnp.bfloat16)

    bufs = ((s0_ref, mx0_ref), (s1_ref, mx1_ref))

    def keys(c, j):
        return k_ref[0, c, pl.ds(pl.multiple_of(j * MOBA_BLOCK, MOBA_BLOCK), MOBA_BLOCK), :]

    def scores(c, t):
        kj = k_ref[0, c, pl.ds(pl.multiple_of(t * MOBA_BLOCK, MOBA_BLOCK), 2 * MOBA_BLOCK), :]
        for g in range(group):
            cols = slice(g * MOBA_BLOCK, (g + 1) * MOBA_BLOCK)
            s = _dot_nt(kj, qs_ref[c, cols, :])
            for par, (s_ref, mx_ref) in enumerate(bufs):
                blk = s[par * MOBA_BLOCK:(par + 1) * MOBA_BLOCK]
                s_ref[c, :, cols] = blk
                mx_ref[c, :, cols] = jnp.max(blk, axis=0, keepdims=True)

    def probs(c, j, s_ref, mx_ref, p_ref):
        bias = bias_ref[c, pl.ds(j, 1), :]
        m_old = m_ref[c]
        m_new = jnp.maximum(m_old, mx_ref[c] + bias)
        m_ref[c] = m_new
        shift = m_new - bias
        for g in range(group):
            cols = slice(g * MOBA_BLOCK, (g + 1) * MOBA_BLOCK)
            p_ref[:, cols] = jnp.exp2(s_ref[c, :, cols] - shift[:, cols]).astype(jnp.bfloat16)
        return jnp.exp2(m_old - m_new)

    def accumulate(c, j, p_ref, alpha):
        acc_ref[c] = acc_ref[c] * alpha + _dot(vt_ref[0, c, j], p_ref[...])

    def select_blocks(c):
        gt = _dot_nt(km_ref[c], qs_ref[c])
        blk = lax.broadcasted_iota(jnp.int32, (nb, rows), 0).astype(jnp.float32)
        past = blk < i.astype(jnp.float32)
        gate = jnp.where(past, gt[0:nb] + gt[nb:2 * nb], NEG)
        chosen = jnp.zeros((nb, rows), jnp.bool_)
        for _ in range(MOBA_TOPK):
            cur = jnp.where(chosen, BELOW_NEG, gate)
            best = jnp.max(cur, axis=0, keepdims=True)
            first = jnp.min(jnp.where(cur == best, blk, float(nb)), axis=0, keepdims=True)
            chosen = chosen | (blk == first)
        bias_ref[c] = jnp.where(chosen & past, 0.0, NEG)

    p_refs = (p0_ref, p1_ref)
    ds, dm, dp = {}, {}, {}

    def own_scores(c):
        ds[c] = _dot_nt(keys(c, i), qs_ref[c]) + cmask_ref[...]
        dm[c] = jnp.max(ds[c], axis=0, keepdims=True)

    def own_probs(c):
        m_ref[c] = dm[c]
        dp[c] = jnp.exp2(ds[c] - dm[c]).astype(jnp.bfloat16)

    def own_out(c):
        acc_ref[c] = _dot(vt_ref[0, c, i], dp[c])

    for c in range(n_kv):
        qs_ref[c] = _stack_heads(q_ref[0, :, c * gw:(c + 1) * gw], group)
    for slot in range(n_kv + 2):
        if slot < n_kv:
            own_scores(slot)
            select_blocks(slot)
            if slot < n_kv - 1:
                scores(slot, 0)
        if 0 <= slot - 1 < n_kv:
            own_probs(slot - 1)
        if 0 <= slot - 2 < n_kv:
            own_out(slot - 2)

    def pair(t, refill):
        pending = None
        for c in range(n_kv):
            if c == 0:
                scores(n_kv - 1, t)
            elif refill:
                scores(c - 1, t + 2)
            for par in range(2):
                alpha = probs(c, t + par, *bufs[par], p_refs[par])
                if pending is not None:
                    accumulate(*pending)
                pending = (c, t + par, p_refs[par], alpha)
        accumulate(*pending)

    n_pairs = (i + 1) // 2

    def steady(u, carry):
        pair(2 * u, True)
        return carry

    lax.fori_loop(0, n_pairs - 1, steady, 0)

    @pl.when(n_pairs > 0)
    def _():
        pair(2 * (n_pairs - 1), False)

    for c in range(n_kv):
        acc = acc_ref[c]
        ot = acc[:HEAD_DIM] / acc[HEAD_DIM:HEAD_DIM + 1]
        o_ref[0, :, c * gw:(c + 1) * gw] = _unstack_heads_t(ot, group, MOBA_BLOCK).astype(jnp.bfloat16)


def _moba_causal_mask(rows):
    key = lax.broadcasted_iota(jnp.int32, (MOBA_BLOCK, rows), 0)
    qpos = lax.broadcasted_iota(jnp.int32, (MOBA_BLOCK, rows), 1) % MOBA_BLOCK
    return jnp.where(key <= qpos, 0.0, NEG).astype(jnp.float32)


def _moba(q, k, vt):
    B, S, _ = q.shape
    n_kv = k.shape[1]
    nb = S // MOBA_BLOCK
    rows = (MIX_WIDTH // n_kv // HEAD_DIM) * MOBA_BLOCK
    return pl.pallas_call(
        _moba_kernel,
        grid=(B, nb),
        in_specs=[
            pl.BlockSpec((1, MOBA_BLOCK, MIX_WIDTH), lambda b, i: (b, i, 0)),
            pl.BlockSpec((1, n_kv, S, HEAD_DIM), lambda b, i: (b, 0, 0, 0)),
            pl.BlockSpec((1, n_kv, nb, V_ROWS, MOBA_BLOCK), lambda b, i: (b, 0, 0, 0, 0)),
            pl.BlockSpec((MOBA_BLOCK, rows), lambda b, i: (0, 0)),
        ],
        out_specs=pl.BlockSpec((1, MOBA_BLOCK, MIX_WIDTH), lambda b, i: (b, i, 0)),
        out_shape=jax.ShapeDtypeStruct((B, S, MIX_WIDTH), jnp.bfloat16),
        scratch_shapes=[
            pltpu.VMEM((n_kv, 2 * nb, HEAD_DIM), jnp.bfloat16),
            pltpu.VMEM((n_kv, rows, HEAD_DIM), jnp.bfloat16),
            pltpu.VMEM((n_kv, nb, rows), jnp.float32),
            pltpu.VMEM((n_kv, MOBA_BLOCK, rows), jnp.float32),
            pltpu.VMEM((n_kv, MOBA_BLOCK, rows), jnp.float32),
            pltpu.VMEM((n_kv, 1, rows), jnp.float32),
            pltpu.VMEM((n_kv, 1, rows), jnp.float32),
            pltpu.VMEM((MOBA_BLOCK, rows), jnp.bfloat16),
            pltpu.VMEM((MOBA_BLOCK, rows), jnp.bfloat16),
            pltpu.VMEM((n_kv, 1, rows), jnp.float32),
            pltpu.VMEM((n_kv, V_ROWS, rows), jnp.float32),
        ],
        compiler_params=pltpu.CompilerParams(
            dimension_semantics=("parallel", "arbitrary"),
            vmem_limit_bytes=VMEM_LIMIT),
        name="moba",
    )(q, k, vt, _moba_causal_mask(rows))


def _rope_tables(seq, q_scale):
    pos = jnp.arange(seq, dtype=jnp.float32)
    inv = ROPE_THETA ** (-jnp.arange(0, HEAD_DIM, 2, dtype=jnp.float32) / HEAD_DIM)
    ang = pos[:, None] * inv[None, :]
    cos, sin = jnp.cos(ang), jnp.sin(ang)
    ck = jnp.tile(cos, (1, LANES // (HEAD_DIM // 2)))
    sk = jnp.tile(jnp.concatenate([-sin, sin], axis=1), (1, LANES // HEAD_DIM))
    return ck * q_scale, sk * q_scale, ck, sk


def kernel(x, norm_g, w_in_a, sinks_a, w_in_b, w_out, final_g):
    B, S, _ = x.shape
    depth = norm_g.shape[0]
    assert S % TOKEN_TILE == 0 and S % SWA_Q_TILE == 0 and S >= 2 * MOBA_BLOCK
    tabs = _rope_tables(S, LOG2E / math.sqrt(HEAD_DIM))

    def layer_params(i):
        g = norm_g[i].reshape(1, D_MODEL)
        if i % 2 == 0:
            return g, w_in_a[i // 2].astype(jnp.bfloat16), N_KV_A, WINDOW
        return g, w_in_b[i // 2].astype(jnp.bfloat16), N_KV_B, MOBA_BLOCK

    h = x
    q, k, vt, sz = _in_proj(h, *layer_params(0)[:2], tabs, *layer_params(0)[2:])
    for i in range(depth):
        if i % 2 == 0:
            group = N_HEADS // N_KV_A
            sinks2 = jnp.repeat(sinks_a[i // 2].astype(jnp.float32) * LOG2E, WINDOW).reshape(
                N_KV_A, 1, group * WINDOW)
            o = _swa(q, k, vt, sinks2)
        else:
            o = _moba(q, k, vt)
        w_o = w_out[i].astype(jnp.bfloat16)
        if i + 1 < depth:
            g, w, n_kv, v_blk = layer_params(i + 1)
            h, q, k, vt, sz = _mid(o, sz, h, w_o, g, w, tabs, n_kv, v_blk)
        else:
            h = _out_proj(o, sz, h, w_o, final_g.reshape(1, D_MODEL))
    return h
```

```python
import functools
import math

import jax
import jax.numpy as jnp
from jax import lax
from jax.experimental import pallas as pl
from jax.experimental.pallas import tpu as pltpu

D_MODEL = 1024
HEAD_DIM = 64
N_HEADS = 16
MIX_WIDTH = N_HEADS * HEAD_DIM
N_KV_A = 2
N_KV_B = 4
WINDOW = 128
MOBA_BLOCK = 256
MOBA_TOPK = 3
ROPE_THETA = 10000.0
NORM_EPS = 1e-5
NEG = -1e30
BELOW_NEG = -3e38
LOG2E = 1.4426950408889634

LANES = 128
MXU_COLS = 256
V_ROWS = HEAD_DIM + 16
TOKEN_TILE = 512
SWA_Q_TILE = 1024
VMEM_LIMIT = 48 * 1024 * 1024

_NT = (((1,), (1,)), ((), ()))


def _dot(a, b):
    return jnp.dot(a, b, preferred_element_type=jnp.float32)


def _dot_nt(a, b):
    return lax.dot_general(a, b, _NT, preferred_element_type=jnp.float32)


def _rope(x, cos, sin_signed, first_half):
    swapped = jnp.where(first_half, pltpu.roll(x, LANES - HEAD_DIM // 2, 1),
                        pltpu.roll(x, HEAD_DIM // 2, 1))
    return x * cos + swapped * sin_signed


def _project(x, g_ref, w_ref, tab_refs, q_ref, k_ref, vt_ref, z_ref, n_kv, v_blk):
    tm = x.shape[0]
    kvw = n_kv * HEAD_DIM
    ms = jnp.mean(x * x, axis=-1, keepdims=True)
    hn = (x * lax.rsqrt(ms + NORM_EPS) * g_ref[...]).astype(jnp.bfloat16)

    lane = lax.broadcasted_iota(jnp.int32, (tm, LANES), 1)
    first_half = (lane % HEAD_DIM) < (HEAD_DIM // 2)
    cq, sq, ck, sk = (t[...] for t in tab_refs)

    for c in range(MIX_WIDTH // MXU_COLS):
        q = _dot(hn, w_ref[:, c * MXU_COLS:(c + 1) * MXU_COLS])
        for hh in range(MXU_COLS // LANES):
            lo = c * MXU_COLS + hh * LANES
            q_ref[0, :, lo:lo + LANES] = _rope(
                q[:, hh * LANES:(hh + 1) * LANES], cq, sq, first_half).astype(jnp.bfloat16)

    kv = _dot(hn, w_ref[:, MIX_WIDTH:MIX_WIDTH + 2 * kvw])
    for c in range(kvw // LANES):
        k = _rope(kv[:, c * LANES:(c + 1) * LANES], ck, sk, first_half).astype(jnp.bfloat16)
        k_ref[0, 2 * c] = k[:, :HEAD_DIM]
        k_ref[0, 2 * c + 1] = k[:, HEAD_DIM:]

    ones = jnp.ones((V_ROWS - HEAD_DIM, v_blk), jnp.bfloat16)
    for c in range(kvw // LANES):
        vt = kv[:, kvw + c * LANES:kvw + (c + 1) * LANES].T.astype(jnp.bfloat16)
        for hh in range(2):
            for b in range(tm // v_blk):
                vt_ref[0, 2 * c + hh, b, 0:HEAD_DIM, :] = vt[hh * HEAD_DIM:(hh + 1) * HEAD_DIM,
                                                              b * v_blk:(b + 1) * v_blk]
                vt_ref[0, 2 * c + hh, b, HEAD_DIM:V_ROWS, :] = ones

    z0 = MIX_WIDTH + 2 * kvw
    for c in range(MIX_WIDTH // MXU_COLS):
        z = _dot(hn, w_ref[:, z0 + c * MXU_COLS:z0 + (c + 1) * MXU_COLS])
        z_ref[0, :, c * MXU_COLS:(c + 1) * MXU_COLS] = (
            z * (1.0 / (1.0 + jnp.exp(-z)))).astype(jnp.bfloat16)


def _in_proj_kernel(n_kv, v_blk, h_ref, g_ref, w_ref, cq_ref, sq_ref, ck_ref, sk_ref,
                    q_ref, k_ref, vt_ref, z_ref):
    _project(h_ref[0], g_ref, w_ref, (cq_ref, sq_ref, ck_ref, sk_ref),
             q_ref, k_ref, vt_ref, z_ref, n_kv, v_blk)


def _mid_kernel(n_kv, v_blk, o_ref, sz_ref, h_ref, wo_ref, g_ref, w_ref,
                cq_ref, sq_ref, ck_ref, sk_ref, hout_ref, q_ref, k_ref, vt_ref, z_ref):
    h = h_ref[0] + _dot(o_ref[0] * sz_ref[0], wo_ref[...])
    hout_ref[0] = h
    _project(h, g_ref, w_ref, (cq_ref, sq_ref, ck_ref, sk_ref),
             q_ref, k_ref, vt_ref, z_ref, n_kv, v_blk)


def _proj_out_specs(B, S, tm, n_kv, v_blk):
    specs = [
        pl.BlockSpec((1, tm, MIX_WIDTH), lambda b, t: (b, t, 0)),
        pl.BlockSpec((1, n_kv, tm, HEAD_DIM), lambda b, t: (b, 0, t, 0)),
        pl.BlockSpec((1, n_kv, tm // v_blk, V_ROWS, v_blk), lambda b, t: (b, 0, t, 0, 0)),
        pl.BlockSpec((1, tm, MIX_WIDTH), lambda b, t: (b, t, 0)),
    ]
    shapes = [
        jax.ShapeDtypeStruct((B, S, MIX_WIDTH), jnp.bfloat16),
        jax.ShapeDtypeStruct((B, n_kv, S, HEAD_DIM), jnp.bfloat16),
        jax.ShapeDtypeStruct((B, n_kv, S // v_blk, V_ROWS, v_blk), jnp.bfloat16),
        jax.ShapeDtypeStruct((B, S, MIX_WIDTH), jnp.bfloat16),
    ]
    return specs, shapes


def _resident(shape):
    return pl.BlockSpec(shape, lambda b, t: (0,) * len(shape), pipeline_mode=pl.Buffered(1))


def _in_proj(h, g, w, tabs, n_kv, v_blk):
    B, S, _ = h.shape
    tm = TOKEN_TILE
    tab_spec = pl.BlockSpec((tm, LANES), lambda b, t: (t, 0))
    out_specs, out_shape = _proj_out_specs(B, S, tm, n_kv, v_blk)
    return pl.pallas_call(
        functools.partial(_in_proj_kernel, n_kv, v_blk),
        grid=(B, S // tm),
        in_specs=[
            pl.BlockSpec((1, tm, D_MODEL), lambda b, t: (b, t, 0)),
            _resident((1, D_MODEL)),
            _resident(w.shape),
            tab_spec, tab_spec, tab_spec, tab_spec,
        ],
        out_specs=out_specs,
        out_shape=out_shape,
        compiler_params=pltpu.CompilerParams(
            dimension_semantics=("parallel", "parallel"), vmem_limit_bytes=VMEM_LIMIT),
        name="in_proj",
    )(h, g, w, *tabs)


def _mid(o, sz, h, w_o, g, w, tabs, n_kv, v_blk):
    B, S, _ = h.shape
    tm = TOKEN_TILE
    row_spec = pl.BlockSpec((1, tm, D_MODEL), lambda b, t: (b, t, 0))
    tab_spec = pl.BlockSpec((tm, LANES), lambda b, t: (t, 0))
    out_specs, out_shape = _proj_out_specs(B, S, tm, n_kv, v_blk)
    return pl.pallas_call(
        functools.partial(_mid_kernel, n_kv, v_blk),
        grid=(B, S // tm),
        in_specs=[
            row_spec, row_spec, row_spec,
            _resident(w_o.shape),
            _resident((1, D_MODEL)),
            _resident(w.shape),
            tab_spec, tab_spec, tab_spec, tab_spec,
        ],
        out_specs=[row_spec] + out_specs,
        out_shape=[jax.ShapeDtypeStruct((B, S, D_MODEL), jnp.float32)] + out_shape,
        compiler_params=pltpu.CompilerParams(
            dimension_semantics=("parallel", "parallel"), vmem_limit_bytes=VMEM_LIMIT),
        name="mid",
    )(o, sz, h, w_o, g, w, *tabs)


def _out_proj_kernel(o_ref, z_ref, h_ref, w_ref, g_ref, out_ref):
    h = h_ref[0] + _dot(o_ref[0] * z_ref[0], w_ref[...])
    ms = jnp.mean(h * h, axis=-1, keepdims=True)
    out_ref[0] = h * lax.rsqrt(ms + NORM_EPS) * g_ref[...]


def _out_proj(o, sz, h, w, g):
    B, S, _ = h.shape
    tm = TOKEN_TILE
    row_spec = pl.BlockSpec((1, tm, D_MODEL), lambda b, t: (b, t, 0))
    return pl.pallas_call(
        _out_proj_kernel,
        grid=(B, S // tm),
        in_specs=[row_spec, row_spec, row_spec, _resident(w.shape), _resident((1, D_MODEL))],
        out_specs=row_spec,
        out_shape=jax.ShapeDtypeStruct((B, S, D_MODEL), jnp.float32),
        compiler_params=pltpu.CompilerParams(
            dimension_semantics=("parallel", "parallel"), vmem_limit_bytes=VMEM_LIMIT),
        name="out_proj",
    )(o, sz, h, w, g)


def _stack_heads(q, n):
    return jnp.concatenate([q[:, g * HEAD_DIM:(g + 1) * HEAD_DIM] for g in range(n)], axis=0)


def _unstack_heads_t(ot, n, rows):
    st = jnp.concatenate([ot[:, g * rows:(g + 1) * rows] for g in range(n)], axis=0)
    return st.T


def _swa_kernel(q_ref, k_ref, vt_ref, sink_ref, mask_ref, o_ref, s_ref, p_ref):
    n = pl.program_id(1)
    n_kv = k_ref.shape[1]
    gw = q_ref.shape[2] // n_kv
    group = gw // HEAD_DIM
    n_sub = q_ref.shape[1] // WINDOW
    units = [(c, sb) for sb in range(n_sub) for c in range(n_kv)]
    mx, m = {}, {}

    def band(sb):
        qb = n * n_sub + sb
        return qb, jnp.maximum(qb - 1, 0)

    def scores(u):
        c, sb = units[u]
        qb, first = band(sb)
        qs = _stack_heads(q_ref[0, sb * WINDOW:(sb + 1) * WINDOW, c * gw:(c + 1) * gw], group)
        kb = k_ref[0, c, pl.ds(pl.multiple_of(first * WINDOW, WINDOW), 2 * WINDOW), :]
        st = _dot_nt(kb, qs) + mask_ref[jnp.where(qb == 0, 1, 0)]
        s_ref[u % 3] = st
        mx[u] = jnp.max(st, axis=0, keepdims=True)

    def probs(u):
        c, _ = units[u]
        m[u] = jnp.maximum(mx[u], sink_ref[c])
        p_ref[u % 2] = jnp.exp2(s_ref[u % 3 + jnp.minimum(n, 0)] - m[u]).astype(jnp.bfloat16)

    def output(u):
        c, sb = units[u]
        _, first = band(sb)
        vt = jnp.concatenate([vt_ref[0, c, first], vt_ref[0, c, first + 1]], axis=1)
        acc = _dot(vt, p_ref[u % 2])
        denom = acc[HEAD_DIM:HEAD_DIM + 1] + jnp.exp2(sink_ref[c] - m[u])
        ot = acc[:HEAD_DIM] / denom
        o_ref[0, sb * WINDOW:(sb + 1) * WINDOW, c * gw:(c + 1) * gw] = _unstack_heads_t(
            ot, group, WINDOW).astype(jnp.bfloat16)

    for slot in range(len(units) + 2):
        if slot < len(units):
            scores(slot)
        if 0 <= slot - 1 < len(units):
            probs(slot - 1)
        if 0 <= slot - 2 < len(units):
            output(slot - 2)


def _swa_mask(rows):
    key = lax.broadcasted_iota(jnp.int32, (2 * WINDOW, rows), 0)
    qpos = lax.broadcasted_iota(jnp.int32, (2 * WINDOW, rows), 1) % WINDOW
    masks = []
    for off in (WINDOW, 0):
        dist = qpos + off - key
        masks.append(jnp.where((dist >= 0) & (dist < WINDOW), 0.0, NEG))
    return jnp.stack(masks).astype(jnp.float32)


def _swa(q, k, vt, sinks2):
    B, S, _ = q.shape
    n_kv = k.shape[1]
    rows = (MIX_WIDTH // n_kv // HEAD_DIM) * WINDOW
    tq = SWA_Q_TILE
    return pl.pallas_call(
        _swa_kernel,
        grid=(B, S // tq),
        in_specs=[
            pl.BlockSpec((1, tq, MIX_WIDTH), lambda b, n: (b, n, 0)),
            pl.BlockSpec((1, n_kv, S, HEAD_DIM), lambda b, n: (b, 0, 0, 0)),
            pl.BlockSpec((1, n_kv, S // WINDOW, V_ROWS, WINDOW), lambda b, n: (b, 0, 0, 0, 0)),
            pl.BlockSpec((n_kv, 1, rows), lambda b, n: (0, 0, 0)),
            pl.BlockSpec((2, 2 * WINDOW, rows), lambda b, n: (0, 0, 0)),
        ],
        out_specs=pl.BlockSpec((1, tq, MIX_WIDTH), lambda b, n: (b, n, 0)),
        out_shape=jax.ShapeDtypeStruct((B, S, MIX_WIDTH), jnp.bfloat16),
        scratch_shapes=[
            pltpu.VMEM((3, 2 * WINDOW, rows), jnp.float32),
            pltpu.VMEM((2, 2 * WINDOW, rows), jnp.bfloat16),
        ],
        compiler_params=pltpu.CompilerParams(
            dimension_semantics=("parallel", "arbitrary"),
            vmem_limit_bytes=VMEM_LIMIT),
        name="swa",
    )(q, k, vt, sinks2, _swa_mask(rows))


def _moba_kernel(q_ref, k_ref, vt_ref, cmask_ref, o_ref, km_ref, qs_ref, bias_ref,
                 s0_ref, s1_ref, mx0_ref, mx1_ref, p0_ref, p1_ref, m_ref, acc_ref):
    i = pl.program_id(1)
    n_kv = k_ref.shape[1]
    nb = k_ref.shape[2] // MOBA_BLOCK
    gw = q_ref.shape[2] // n_kv
    group = gw // HEAD_DIM
    rows = group * MOBA_BLOCK
    STRIPS = MOBA_BLOCK // LANES

    @pl.when(i == 0)
    def _():
        for c in range(n_kv):
            means = [jnp.sum(k_ref[0, c, j * MOBA_BLOCK:(j + 1) * MOBA_BLOCK, :].astype(jnp.float32),
                             axis=0, keepdims=True) for j in range(nb)]
            km = jnp.concatenate(means, axis=0) * (1.0 / MOBA_BLOCK)
            hi = km.astype(jnp.bfloat16)
            km_ref[c, 0:nb] = hi
            km_ref[c, nb:2 * nb] = (km - hi.astype(jnp.float32)).astype(jnp.bfloat16)

    bufs = ((s0_ref, mx0_ref), (s1_ref, mx1_ref))

    def keys(c, j):
        return k_ref[0, c, pl.ds(pl.multiple_of(j * MOBA_BLOCK, MOBA_BLOCK), MOBA_BLOCK), :]

    def scores(c, t):
        kj = k_ref[0, c, pl.ds(pl.multiple_of(t * MOBA_BLOCK, MOBA_BLOCK), 2 * MOBA_BLOCK), :]
        for g in range(group):
            cols = slice(g * MOBA_BLOCK, (g + 1) * MOBA_BLOCK)
            s = _dot_nt(kj, qs_ref[c, cols, :])
            for par, (s_ref, mx_ref) in enumerate(bufs):
                blk = s[par * MOBA_BLOCK:(par + 1) * MOBA_BLOCK]
                for h in range(STRIPS):
                    s_ref[c, g * STRIPS + h] = blk[:, h * LANES:(h + 1) * LANES]
                mx_ref[c, :, cols] = jnp.max(blk, axis=0, keepdims=True)

    def probs(c, j, s_ref, mx_ref, p_ref):
        bias = bias_ref[c, pl.ds(j, 1), :]
        m_old = m_ref[c]
        m_new = jnp.maximum(m_old, mx_ref[c] + bias)
        m_ref[c] = m_new
        shift = m_new - bias
        for g in range(group):
            cols = slice(g * MOBA_BLOCK, (g + 1) * MOBA_BLOCK)
            for h in range(STRIPS):
                k = g * STRIPS + h
                p_ref[k] = jnp.exp2(s_ref[c, k] - shift[:, k * LANES:(k + 1) * LANES]).astype(jnp.bfloat16)
        return jnp.exp2(m_old - m_new)

    def accumulate(c, j, p_ref, alpha):
        p = jnp.concatenate([p_ref[k] for k in range(p_ref.shape[0])], axis=1)
        acc_ref[c] = acc_ref[c] * alpha + _dot(vt_ref[0, c, j], p)

    def select_blocks(c):
        gt = _dot_nt(km_ref[c], qs_ref[c])
        blk = lax.broadcasted_iota(jnp.int32, (nb, rows), 0).astype(jnp.float32)
        past = blk < i.astype(jnp.float32)
        gate = jnp.where(past, gt[0:nb] + gt[nb:2 * nb], NEG)
        chosen = jnp.zeros((nb, rows), jnp.bool_)
        for _ in range(MOBA_TOPK):
            cur = jnp.where(chosen, BELOW_NEG, gate)
            best = jnp.max(cur, axis=0, keepdims=True)
            first = jnp.min(jnp.where(cur == best, blk, float(nb)), axis=0, keepdims=True)
            chosen = chosen | (blk == first)
        bias_ref[c] = jnp.where(chosen & past, 0.0, NEG)

    p_refs = (p0_ref, p1_ref)
    ds, dm, dp = {}, {}, {}

    def own_scores(c):
        ds[c] = _dot_nt(keys(c, i), qs_ref[c]) + cmask_ref[...]
        dm[c] = jnp.max(ds[c], axis=0, keepdims=True)

    def own_probs(c):
        m_ref[c] = dm[c]
        dp[c] = jnp.exp2(ds[c] - dm[c]).astype(jnp.bfloat16)

    def own_out(c):
        acc_ref[c] = _dot(vt_ref[0, c, i], dp[c])

    for c in range(n_kv):
        qs_ref[c] = _stack_heads(q_ref[0, :, c * gw:(c + 1) * gw], group)
    for slot in range(n_kv + 2):
        if slot < n_kv:
            own_scores(slot)
            select_blocks(slot)
            if slot < n_kv - 1:
                scores(slot, 0)
        if 0 <= slot - 1 < n_kv:
            own_probs(slot - 1)
        if 0 <= slot - 2 < n_kv:
            own_out(slot - 2)

    def pair(t, refill):
        pending = None
        for c in range(n_kv):
            if c == 0:
                scores(n_kv - 1, t)
            elif refill:
                scores(c - 1, t + 2)
            for par in range(2):
                alpha = probs(c, t + par, *bufs[par], p_refs[par])
                if pending is not None:
                    accumulate(*pending)
                pending = (c, t + par, p_refs[par], alpha)
        accumulate(*pending)

    n_pairs = (i + 1) // 2

    def steady(u, carry):
        pair(2 * u, True)
        return carry

    lax.fori_loop(0, n_pairs - 1, steady, 0)

    @pl.when(n_pairs > 0)
    def _():
        pair(2 * (n_pairs - 1), False)

    for c in range(n_kv):
        acc = acc_ref[c]
        ot = acc[:HEAD_DIM] / acc[HEAD_DIM:HEAD_DIM + 1]
        o_ref[0, :, c * gw:(c + 1) * gw] = _unstack_heads_t(ot, group, MOBA_BLOCK).astype(jnp.bfloat16)


def _moba_causal_mask(rows):
    key = lax.broadcasted_iota(jnp.int32, (MOBA_BLOCK, rows), 0)
    qpos = lax.broadcasted_iota(jnp.int32, (MOBA_BLOCK, rows), 1) % MOBA_BLOCK
    return jnp.where(key <= qpos, 0.0, NEG).astype(jnp.float32)


def _moba(q, k, vt):
    B, S, _ = q.shape
    n_kv = k.shape[1]
    nb = S // MOBA_BLOCK
    rows = (MIX_WIDTH // n_kv // HEAD_DIM) * MOBA_BLOCK
    return pl.pallas_call(
        _moba_kernel,
        grid=(B, nb),
        in_specs=[
            pl.BlockSpec((1, MOBA_BLOCK, MIX_WIDTH), lambda b, i: (b, i, 0)),
            pl.BlockSpec((1, n_kv, S, HEAD_DIM), lambda b, i: (b, 0, 0, 0)),
            pl.BlockSpec((1, n_kv, nb, V_ROWS, MOBA_BLOCK), lambda b, i: (b, 0, 0, 0, 0)),
            pl.BlockSpec((MOBA_BLOCK, rows), lambda b, i: (0, 0)),
        ],
        out_specs=pl.BlockSpec((1, MOBA_BLOCK, MIX_WIDTH), lambda b, i: (b, i, 0)),
        out_shape=jax.ShapeDtypeStruct((B, S, MIX_WIDTH), jnp.bfloat16),
        scratch_shapes=[
            pltpu.VMEM((n_kv, 2 * nb, HEAD_DIM), jnp.bfloat16),
            pltpu.VMEM((n_kv, rows, HEAD_DIM), jnp.bfloat16),
            pltpu.VMEM((n_kv, nb, rows), jnp.float32),
            pltpu.VMEM((n_kv, rows // LANES, MOBA_BLOCK, LANES), jnp.float32),
            pltpu.VMEM((n_kv, rows // LANES, MOBA_BLOCK, LANES), jnp.float32),
            pltpu.VMEM((n_kv, 1, rows), jnp.float32),
            pltpu.VMEM((n_kv, 1, rows), jnp.float32),
            pltpu.VMEM((rows // LANES, MOBA_BLOCK, LANES), jnp.bfloat16),
            pltpu.VMEM((rows // LANES, MOBA_BLOCK, LANES), jnp.bfloat16),
            pltpu.VMEM((n_kv, 1, rows), jnp.float32),
            pltpu.VMEM((n_kv, V_ROWS, rows), jnp.float32),
        ],
        compiler_params=pltpu.CompilerParams(
            dimension_semantics=("parallel", "arbitrary"),
            vmem_limit_bytes=VMEM_LIMIT),
        name="moba",
    )(q, k, vt, _moba_causal_mask(rows))


def _rope_tables(seq, q_scale):
    pos = jnp.arange(seq, dtype=jnp.float32)
    inv = ROPE_THETA ** (-jnp.arange(0, HEAD_DIM, 2, dtype=jnp.float32) / HEAD_DIM)
    ang = pos[:, None] * inv[None, :]
    cos, sin = jnp.cos(ang), jnp.sin(ang)
    ck = jnp.tile(cos, (1, LANES // (HEAD_DIM // 2)))
    sk = jnp.tile(jnp.concatenate([-sin, sin], axis=1), (1, LANES // HEAD_DIM))
    return ck * q_scale, sk * q_scale, ck, sk


def kernel(x, norm_g, w_in_a, sinks_a, w_in_b, w_out, final_g):
    B, S, _ = x.shape
    depth = norm_g.shape[0]
    assert S % TOKEN_TILE == 0 and S % SWA_Q_TILE == 0 and S >= 2 * MOBA_BLOCK
    tabs = _rope_tables(S, LOG2E / math.sqrt(HEAD_DIM))

    def layer_params(i):
        g = norm_g[i].reshape(1, D_MODEL)
        if i % 2 == 0:
            return g, w_in_a[i // 2].astype(jnp.bfloat16), N_KV_A, WINDOW
        return g, w_in_b[i // 2].astype(jnp.bfloat16), N_KV_B, MOBA_BLOCK

    h = x
    q, k, vt, sz = _in_proj(h, *layer_params(0)[:2], tabs, *layer_params(0)[2:])
    for i in range(depth):
        if i % 2 == 0:
            group = N_HEADS // N_KV_A
            sinks2 = jnp.repeat(sinks_a[i // 2].astype(jnp.float32) * LOG2E, WINDOW).reshape(
                N_KV_A, 1, group * WINDOW)
            o = _swa(q, k, vt, sinks2)
        else:
            o = _moba(q, k, vt)
        w_o = w_out[i].astype(jnp.bfloat16)
        if i + 1 < depth:
            g, w, n_kv, v_blk = layer_params(i + 1)
            h, q, k, vt, sz = _mid(o, sz, h, w_o, g, w, tabs, n_kv, v_blk)
        else:
            h = _out_proj(o, sz, h, w_o, final_g.reshape(1, D_MODEL))
    return h
```

```python
import functools
import math

import jax
import jax.numpy as jnp
from jax import lax
from jax.experimental import pallas as pl
from jax.experimental.pallas import tpu as pltpu

D_MODEL = 1024
HEAD_DIM = 64
N_HEADS = 16
MIX_WIDTH = N_HEADS * HEAD_DIM
N_KV_A = 2
N_KV_B = 4
WINDOW = 128
MOBA_BLOCK = 256
MOBA_TOPK = 3
ROPE_THETA = 10000.0
NORM_EPS = 1e-5
NEG = -1e30
BELOW_NEG = -3e38
LOG2E = 1.4426950408889634

LANES = 128
MXU_COLS = 256
V_ROWS = HEAD_DIM + 16
TOKEN_TILE = 512
SWA_Q_TILE = 1024
VMEM_LIMIT = 48 * 1024 * 1024

_NT = (((1,), (1,)), ((), ()))


def _dot(a, b):
    return jnp.dot(a, b, preferred_element_type=jnp.float32)


def _dot_nt(a, b):
    return lax.dot_general(a, b, _NT, preferred_element_type=jnp.float32)


def _rope(x, cos, sin_signed, first_half):
    swapped = jnp.where(first_half, pltpu.roll(x, LANES - HEAD_DIM // 2, 1),
                        pltpu.roll(x, HEAD_DIM // 2, 1))
    return x * cos + swapped * sin_signed


def _project(x, g_ref, w_ref, tab_refs, q_ref, k_ref, vt_ref, z_ref, n_kv, v_blk):
    tm = x.shape[0]
    kvw = n_kv * HEAD_DIM
    ms = jnp.mean(x * x, axis=-1, keepdims=True)
    hn = (x * lax.rsqrt(ms + NORM_EPS) * g_ref[...]).astype(jnp.bfloat16)

    lane = lax.broadcasted_iota(jnp.int32, (tm, LANES), 1)
    first_half = (lane % HEAD_DIM) < (HEAD_DIM // 2)
    cq, sq, ck, sk = (t[...] for t in tab_refs)

    for c in range(MIX_WIDTH // MXU_COLS):
        q = _dot(hn, w_ref[:, c * MXU_COLS:(c + 1) * MXU_COLS])
        for hh in range(MXU_COLS // LANES):
            lo = c * MXU_COLS + hh * LANES
            q_ref[0, :, lo:lo + LANES] = _rope(
                q[:, hh * LANES:(hh + 1) * LANES], cq, sq, first_half).astype(jnp.bfloat16)

    kv = _dot(hn, w_ref[:, MIX_WIDTH:MIX_WIDTH + 2 * kvw])
    for c in range(kvw // LANES):
        k = _rope(kv[:, c * LANES:(c + 1) * LANES], ck, sk, first_half).astype(jnp.bfloat16)
        k_ref[0, 2 * c] = k[:, :HEAD_DIM]
        k_ref[0, 2 * c + 1] = k[:, HEAD_DIM:]

    ones = jnp.ones((V_ROWS - HEAD_DIM, v_blk), jnp.bfloat16)
    for c in range(kvw // LANES):
        vt = kv[:, kvw + c * LANES:kvw + (c + 1) * LANES].T.astype(jnp.bfloat16)
        for hh in range(2):
            for b in range(tm // v_blk):
                vt_ref[0, 2 * c + hh, b, 0:HEAD_DIM, :] = vt[hh * HEAD_DIM:(hh + 1) * HEAD_DIM,
                                                              b * v_blk:(b + 1) * v_blk]
                vt_ref[0, 2 * c + hh, b, HEAD_DIM:V_ROWS, :] = ones

    z0 = MIX_WIDTH + 2 * kvw
    for c in range(MIX_WIDTH // MXU_COLS):
        z = _dot(hn, w_ref[:, z0 + c * MXU_COLS:z0 + (c + 1) * MXU_COLS])
        z_ref[0, :, c * MXU_COLS:(c + 1) * MXU_COLS] = (
            z * (1.0 / (1.0 + jnp.exp(-z)))).astype(jnp.bfloat16)


def _in_proj_kernel(n_kv, v_blk, h_ref, g_ref, w_ref, cq_ref, sq_ref, ck_ref, sk_ref,
                    q_ref, k_ref, vt_ref, z_ref):
    _project(h_ref[0], g_ref, w_ref, (cq_ref, sq_ref, ck_ref, sk_ref),
             q_ref, k_ref, vt_ref, z_ref, n_kv, v_blk)


def _mid_kernel(n_kv, v_blk, o_ref, sz_ref, h_ref, wo_ref, g_ref, w_ref,
                cq_ref, sq_ref, ck_ref, sk_ref, hout_ref, q_ref, k_ref, vt_ref, z_ref):
    h = h_ref[0] + _dot(o_ref[0] * sz_ref[0], wo_ref[...])
    hout_ref[0] = h
    _project(h, g_ref, w_ref, (cq_ref, sq_ref, ck_ref, sk_ref),
             q_ref, k_ref, vt_ref, z_ref, n_kv, v_blk)


def _proj_out_specs(B, S, tm, n_kv, v_blk):
    specs = [
        pl.BlockSpec((1, tm, MIX_WIDTH), lambda b, t: (b, t, 0)),
        pl.BlockSpec((1, n_kv, tm, HEAD_DIM), lambda b, t: (b, 0, t, 0)),
        pl.BlockSpec((1, n_kv, tm // v_blk, V_ROWS, v_blk), lambda b, t: (b, 0, t, 0, 0)),
        pl.BlockSpec((1, tm, MIX_WIDTH), lambda b, t: (b, t, 0)),
    ]
    shapes = [
        jax.ShapeDtypeStruct((B, S, MIX_WIDTH), jnp.bfloat16),
        jax.ShapeDtypeStruct((B, n_kv, S, HEAD_DIM), jnp.bfloat16),
        jax.ShapeDtypeStruct((B, n_kv, S // v_blk, V_ROWS, v_blk), jnp.bfloat16),
        jax.ShapeDtypeStruct((B, S, MIX_WIDTH), jnp.bfloat16),
    ]
    return specs, shapes


def _resident(shape):
    return pl.BlockSpec(shape, lambda b, t: (0,) * len(shape), pipeline_mode=pl.Buffered(1))


def _in_proj(h, g, w, tabs, n_kv, v_blk):
    B, S, _ = h.shape
    tm = TOKEN_TILE
    tab_spec = pl.BlockSpec((tm, LANES), lambda b, t: (t, 0))
    out_specs, out_shape = _proj_out_specs(B, S, tm, n_kv, v_blk)
    return pl.pallas_call(
        functools.partial(_in_proj_kernel, n_kv, v_blk),
        grid=(B, S // tm),
        in_specs=[
            pl.BlockSpec((1, tm, D_MODEL), lambda b, t: (b, t, 0)),
            _resident((1, D_MODEL)),
            _resident(w.shape),
            tab_spec, tab_spec, tab_spec, tab_spec,
        ],
        out_specs=out_specs,
        out_shape=out_shape,
        compiler_params=pltpu.CompilerParams(
            dimension_semantics=("parallel", "parallel"), vmem_limit_bytes=VMEM_LIMIT),
        name="in_proj",
    )(h, g, w, *tabs)


def _mid(o, sz, h, w_o, g, w, tabs, n_kv, v_blk):
    B, S, _ = h.shape
    tm = TOKEN_TILE
    row_spec = pl.BlockSpec((1, tm, D_MODEL), lambda b, t: (b, t, 0))
    tab_spec = pl.BlockSpec((tm, LANES), lambda b, t: (t, 0))
    out_specs, out_shape = _proj_out_specs(B, S, tm, n_kv, v_blk)
    return pl.pallas_call(
        functools.partial(_mid_kernel, n_kv, v_blk),
        grid=(B, S // tm),
        in_specs=[
            row_spec, row_spec, row_spec,
            _resident(w_o.shape),
            _resident((1, D_MODEL)),
            _resident(w.shape),
            tab_spec, tab_spec, tab_spec, tab_spec,
        ],
        out_specs=[row_spec] + out_specs,
        out_shape=[jax.ShapeDtypeStruct((B, S, D_MODEL), jnp.float32)] + out_shape,
        compiler_params=pltpu.CompilerParams(
            dimension_semantics=("parallel", "parallel"), vmem_limit_bytes=VMEM_LIMIT),
        name="mid",
    )(o, sz, h, w_o, g, w, *tabs)


def _out_proj_kernel(o_ref, z_ref, h_ref, w_ref, g_ref, out_ref):
    h = h_ref[0] + _dot(o_ref[0] * z_ref[0], w_ref[...])
    ms = jnp.mean(h * h, axis=-1, keepdims=True)
    out_ref[0] = h * lax.rsqrt(ms + NORM_EPS) * g_ref[...]


def _out_proj(o, sz, h, w, g):
    B, S, _ = h.shape
    tm = TOKEN_TILE
    row_spec = pl.BlockSpec((1, tm, D_MODEL), lambda b, t: (b, t, 0))
    return pl.pallas_call(
        _out_proj_kernel,
        grid=(B, S // tm),
        in_specs=[row_spec, row_spec, row_spec, _resident(w.shape), _resident((1, D_MODEL))],
        out_specs=row_spec,
        out_shape=jax.ShapeDtypeStruct((B, S, D_MODEL), jnp.float32),
        compiler_params=pltpu.CompilerParams(
            dimension_semantics=("parallel", "parallel"), vmem_limit_bytes=VMEM_LIMIT),
        name="out_proj",
    )(o, sz, h, w, g)


def _stack_heads(q, n):
    return jnp.concatenate([q[:, g * HEAD_DIM:(g + 1) * HEAD_DIM] for g in range(n)], axis=0)


def _unstack_heads_t(ot, n, rows):
    st = jnp.concatenate([ot[:, g * rows:(g + 1) * rows] for g in range(n)], axis=0)
    return st.T


def _swa_kernel(q_ref, k_ref, vt_ref, sink_ref, mask_ref, o_ref, s_ref, p_ref):
    n = pl.program_id(1)
    n_kv = k_ref.shape[1]
    gw = q_ref.shape[2] // n_kv
    group = gw // HEAD_DIM
    n_sub = q_ref.shape[1] // WINDOW
    units = [(c, sb) for sb in range(n_sub) for c in range(n_kv)]
    mx, m = {}, {}

    def band(sb):
        qb = n * n_sub + sb
        return qb, jnp.maximum(qb - 1, 0)

    def scores(u):
        c, sb = units[u]
        qb, first = band(sb)
        qs = _stack_heads(q_ref[0, sb * WINDOW:(sb + 1) * WINDOW, c * gw:(c + 1) * gw], group)
        kb = k_ref[0, c, pl.ds(pl.multiple_of(first * WINDOW, WINDOW), 2 * WINDOW), :]
        st = _dot_nt(kb, qs) + mask_ref[jnp.where(qb == 0, 1, 0)]
        s_ref[u % 3] = st
        mx[u] = jnp.max(st, axis=0, keepdims=True)

    def probs(u):
        c, _ = units[u]
        m[u] = jnp.maximum(mx[u], sink_ref[c])
        p_ref[u % 2] = jnp.exp2(s_ref[u % 3 + jnp.minimum(n, 0)] - m[u]).astype(jnp.bfloat16)

    def output(u):
        c, sb = units[u]
        _, first = band(sb)
        vt = jnp.concatenate([vt_ref[0, c, first], vt_ref[0, c, first + 1]], axis=1)
        acc = _dot(vt, p_ref[u % 2])
        denom = acc[HEAD_DIM:HEAD_DIM + 1] + jnp.exp2(sink_ref[c] - m[u])
        ot = acc[:HEAD_DIM] / denom
        o_ref[0, sb * WINDOW:(sb + 1) * WINDOW, c * gw:(c + 1) * gw] = _unstack_heads_t(
            ot, group, WINDOW).astype(jnp.bfloat16)

    for slot in range(len(units) + 2):
        if slot < len(units):
            scores(slot)
        if 0 <= slot - 1 < len(units):
            probs(slot - 1)
        if 0 <= slot - 2 < len(units):
            output(slot - 2)


def _swa_mask(rows):
    key = lax.broadcasted_iota(jnp.int32, (2 * WINDOW, rows), 0)
    qpos = lax.broadcasted_iota(jnp.int32, (2 * WINDOW, rows), 1) % WINDOW
    masks = []
    for off in (WINDOW, 0):
        dist = qpos + off - key
        masks.append(jnp.where((dist >= 0) & (dist < WINDOW), 0.0, NEG))
    return jnp.stack(masks).astype(jnp.float32)


def _swa(q, k, vt, sinks2):
    B, S, _ = q.shape
    n_kv = k.shape[1]
    rows = (MIX_WIDTH // n_kv // HEAD_DIM) * WINDOW
    tq = SWA_Q_TILE
    return pl.pallas_call(
        _swa_kernel,
        grid=(B, S // tq),
        in_specs=[
            pl.BlockSpec((1, tq, MIX_WIDTH), lambda b, n: (b, n, 0)),
            pl.BlockSpec((1, n_kv, S, HEAD_DIM), lambda b, n: (b, 0, 0, 0)),
            pl.BlockSpec((1, n_kv, S // WINDOW, V_ROWS, WINDOW), lambda b, n: (b, 0, 0, 0, 0)),
            pl.BlockSpec((n_kv, 1, rows), lambda b, n: (0, 0, 0)),
            pl.BlockSpec((2, 2 * WINDOW, rows), lambda b, n: (0, 0, 0)),
        ],
        out_specs=pl.BlockSpec((1, tq, MIX_WIDTH), lambda b, n: (b, n, 0)),
        out_shape=jax.ShapeDtypeStruct((B, S, MIX_WIDTH), jnp.bfloat16),
        scratch_shapes=[
            pltpu.VMEM((3, 2 * WINDOW, rows), jnp.float32),
            pltpu.VMEM((2, 2 * WINDOW, rows), jnp.bfloat16),
        ],
        compiler_params=pltpu.CompilerParams(
            dimension_semantics=("parallel", "arbitrary"),
            vmem_limit_bytes=VMEM_LIMIT),
        name="swa",
    )(q, k, vt, sinks2, _swa_mask(rows))


def _moba_kernel(q_ref, k_ref, vt_ref, cmask_ref, o_ref, km_ref, qs_ref, bias_ref,
                 s0_ref, s1_ref, mx0_ref, mx1_ref, p0_ref, p1_ref, m_ref, acc_ref):
    i = pl.program_id(1)
    n_kv = k_ref.shape[1]
    nb = k_ref.shape[2] // MOBA_BLOCK
    gw = q_ref.shape[2] // n_kv
    group = gw // HEAD_DIM
    rows = group * MOBA_BLOCK

    @pl.when(i == 0)
    def _():
        for c in range(n_kv):
            means = [jnp.sum(k_ref[0, c, j * MOBA_BLOCK:(j + 1) * MOBA_BLOCK, :].astype(jnp.float32),
                             axis=0, keepdims=True) for j in range(nb)]
            km = jnp.concatenate(means, axis=0) * (1.0 / MOBA_BLOCK)
            hi = km.astype(jnp.bfloat16)
            km_ref[c, 0:nb] = hi
            km_ref[c, nb:2 * nb] = (km - hi.astype(jnp.float32)).astype(jnp.bfloat16)

    bufs = ((s0_ref, mx0_ref), (s1_ref, mx1_ref))

    def keys(c, j):
        return k_ref[0, c, pl.ds(pl.multiple_of(j * MOBA_BLOCK, MOBA_BLOCK), MOBA_BLOCK), :]

    def scores(c, t):
        kj = k_ref[0, c, pl.ds(pl.multiple_of(t * MOBA_BLOCK, MOBA_BLOCK), 2 * MOBA_BLOCK), :]
        for g in range(group):
            cols = slice(g * MOBA_BLOCK, (g + 1) * MOBA_BLOCK)
            s = _dot_nt(kj, qs_ref[c, cols, :])
            for par, (s_ref, mx_ref) in enumerate(bufs):
                blk = s[par * MOBA_BLOCK:(par + 1) * MOBA_BLOCK]
                s_ref[c, :, cols] = blk
                mx_ref[c, :, cols] = jnp.max(blk, axis=0, keepdims=True)

    def update(c, j, s_ref, mx_ref):
        bias = bias_ref[c, pl.ds(j, 1), :]
        m_old = m_ref[c]
        m_new = jnp.maximum(m_old, mx_ref[c] + bias)
        m_ref[c] = m_new
        shift = m_new - bias
        alpha = jnp.exp2(m_old - m_new)
        vt = vt_ref[0, c, j]
        for g in range(group):
            cols = slice(g * MOBA_BLOCK, (g + 1) * MOBA_BLOCK)
            p = jnp.exp2(s_ref[c, :, cols] - shift[:, cols]).astype(jnp.bfloat16)
            acc_ref[c, :, cols] = acc_ref[c, :, cols] * alpha[:, cols] + _dot(vt, p)

    def select_blocks(c):
        gt = _dot_nt(km_ref[c], qs_ref[c])
        blk = lax.broadcasted_iota(jnp.int32, (nb, rows), 0).astype(jnp.float32)
        past = blk < i.astype(jnp.float32)
        gate = jnp.where(past, gt[0:nb] + gt[nb:2 * nb], NEG)
        chosen = jnp.zeros((nb, rows), jnp.bool_)
        for _ in range(MOBA_TOPK):
            cur = jnp.where(chosen, BELOW_NEG, gate)
            best = jnp.max(cur, axis=0, keepdims=True)
            first = jnp.min(jnp.where(cur == best, blk, float(nb)), axis=0, keepdims=True)
            chosen = chosen | (blk == first)
        bias_ref[c] = jnp.where(chosen & past, 0.0, NEG)

    p_refs = (p0_ref, p1_ref)
    ds, dm, dp = {}, {}, {}

    def own_scores(c):
        ds[c] = _dot_nt(keys(c, i), qs_ref[c]) + cmask_ref[...]
        dm[c] = jnp.max(ds[c], axis=0, keepdims=True)

    def own_probs(c):
        m_ref[c] = dm[c]
        dp[c] = jnp.exp2(ds[c] - dm[c]).astype(jnp.bfloat16)

    def own_out(c):
        acc_ref[c] = _dot(vt_ref[0, c, i], dp[c])

    for c in range(n_kv):
        qs_ref[c] = _stack_heads(q_ref[0, :, c * gw:(c + 1) * gw], group)
    for slot in range(n_kv + 2):
        if slot < n_kv:
            own_scores(slot)
            select_blocks(slot)
            if slot < n_kv - 1:
                scores(slot, 0)
        if 0 <= slot - 1 < n_kv:
            own_probs(slot - 1)
        if 0 <= slot - 2 < n_kv:
            own_out(slot - 2)

    def pair(t, refill):
        for c in range(n_kv):
            if c == 0:
                scores(n_kv - 1, t)
            elif refill:
                scores(c - 1, t + 2)
            for par in range(2):
                update(c, t + par, *bufs[par])

    n_pairs = (i + 1) // 2

    def steady(u, carry):
        pair(2 * u, True)
        return carry

    lax.fori_loop(0, n_pairs - 1, steady, 0)

    @pl.when(n_pairs > 0)
    def _():
        pair(2 * (n_pairs - 1), False)

    for c in range(n_kv):
        acc = acc_ref[c]
        ot = acc[:HEAD_DIM] / acc[HEAD_DIM:HEAD_DIM + 1]
        o_ref[0, :, c * gw:(c + 1) * gw] = _unstack_heads_t(ot, group, MOBA_BLOCK).astype(jnp.bfloat16)


def _moba_causal_mask(rows):
    key = lax.broadcasted_iota(jnp.int32, (MOBA_BLOCK, rows), 0)
    qpos = lax.broadcasted_iota(jnp.int32, (MOBA_BLOCK, rows), 1) % MOBA_BLOCK
    return jnp.where(key <= qpos, 0.0, NEG).astype(jnp.float32)


def _moba(q, k, vt):
    B, S, _ = q.shape
    n_kv = k.shape[1]
    nb = S // MOBA_BLOCK
    rows = (MIX_WIDTH // n_kv // HEAD_DIM) * MOBA_BLOCK
    return pl.pallas_call(
        _moba_kernel,
        grid=(B, nb),
        in_specs=[
            pl.BlockSpec((1, MOBA_BLOCK, MIX_WIDTH), lambda b, i: (b, i, 0)),
            pl.BlockSpec((1, n_kv, S, HEAD_DIM), lambda b, i: (b, 0, 0, 0)),
            pl.BlockSpec((1, n_kv, nb, V_ROWS, MOBA_BLOCK), lambda b, i: (b, 0, 0, 0, 0)),
            pl.BlockSpec((MOBA_BLOCK, rows), lambda b, i: (0, 0)),
        ],
        out_specs=pl.BlockSpec((1, MOBA_BLOCK, MIX_WIDTH), lambda b, i: (b, i, 0)),
        out_shape=jax.ShapeDtypeStruct((B, S, MIX_WIDTH), jnp.bfloat16),
        scratch_shapes=[
            pltpu.VMEM((n_kv, 2 * nb, HEAD_DIM), jnp.bfloat16),
            pltpu.VMEM((n_kv, rows, HEAD_DIM), jnp.bfloat16),
            pltpu.VMEM((n_kv, nb, rows), jnp.float32),
            pltpu.VMEM((n_kv, MOBA_BLOCK, rows), jnp.float32),
            pltpu.VMEM((n_kv, MOBA_BLOCK, rows), jnp.float32),
            pltpu.VMEM((n_kv, 1, rows), jnp.float32),
            pltpu.VMEM((n_kv, 1, rows), jnp.float32),
            pltpu.VMEM((MOBA_BLOCK, rows), jnp.bfloat16),
            pltpu.VMEM((MOBA_BLOCK, rows), jnp.bfloat16),
            pltpu.VMEM((n_kv, 1, rows), jnp.float32),
            pltpu.VMEM((n_kv, V_ROWS, rows), jnp.float32),
        ],
        compiler_params=pltpu.CompilerParams(
            dimension_semantics=("parallel", "arbitrary"),
            vmem_limit_bytes=VMEM_LIMIT),
        name="moba",
    )(q, k, vt, _moba_causal_mask(rows))


def _rope_tables(seq, q_scale):
    pos = jnp.arange(seq, dtype=jnp.float32)
    inv = ROPE_THETA ** (-jnp.arange(0, HEAD_DIM, 2, dtype=jnp.float32) / HEAD_DIM)
    ang = pos[:, None] * inv[None, :]
    cos, sin = jnp.cos(ang), jnp.sin(ang)
    ck = jnp.tile(cos, (1, LANES // (HEAD_DIM // 2)))
    sk = jnp.tile(jnp.concatenate([-sin, sin], axis=1), (1, LANES // HEAD_DIM))
    return ck * q_scale, sk * q_scale, ck, sk


def kernel(x, norm_g, w_in_a, sinks_a, w_in_b, w_out, final_g):
    B, S, _ = x.shape
    depth = norm_g.shape[0]
    assert S % TOKEN_TILE == 0 and S % SWA_Q_TILE == 0 and S >= 2 * MOBA_BLOCK
    tabs = _rope_tables(S, LOG2E / math.sqrt(HEAD_DIM))

    def layer_params(i):
        g = norm_g[i].reshape(1, D_MODEL)
        if i % 2 == 0:
            return g, w_in_a[i // 2].astype(jnp.bfloat16), N_KV_A, WINDOW
        return g, w_in_b[i // 2].astype(jnp.bfloat16), N_KV_B, MOBA_BLOCK

    h = x
    q, k, vt, sz = _in_proj(h, *layer_params(0)[:2], tabs, *layer_params(0)[2:])
    for i in range(depth):
        if i % 2 == 0:
            group = N_HEADS // N_KV_A
            sinks2 = jnp.repeat(sinks_a[i // 2].astype(jnp.float32) * LOG2E, WINDOW).reshape(
                N_KV_A, 1, group * WINDOW)
            o = _swa(q, k, vt, sinks2)
        else:
            o = _moba(q, k, vt)
        w_o = w_out[i].astype(jnp.bfloat16)
        if i + 1 < depth:
            g, w, n_kv, v_blk = layer_params(i + 1)
            h, q, k, vt, sz = _mid(o, sz, h, w_o, g, w, tabs, n_kv, v_blk)
        else:
            h = _out_proj(o, sz, h, w_o, final_g.reshape(1, D_MODEL))
    return h
```

```python
import functools
import math

import jax
import jax.numpy as jnp
from jax import lax
from jax.experimental import pallas as pl
from jax.experimental.pallas import tpu as pltpu

D_MODEL = 1024
HEAD_DIM = 64
N_HEADS = 16
MIX_WIDTH = N_HEADS * HEAD_DIM
N_KV_A = 2
N_KV_B = 4
WINDOW = 128
MOBA_BLOCK = 256
MOBA_TOPK = 3
ROPE_THETA = 10000.0
NORM_EPS = 1e-5
NEG = -1e30
BELOW_NEG = -3e38
LOG2E = 1.4426950408889634

LANES = 128
MXU_COLS = 256
V_ROWS = HEAD_DIM + 16
TOKEN_TILE = 512
SWA_Q_TILE = 1024
VMEM_LIMIT = 48 * 1024 * 1024

_NT = (((1,), (1,)), ((), ()))


def _dot(a, b):
    return jnp.dot(a, b, preferred_element_type=jnp.float32)


def _dot_nt(a, b):
    return lax.dot_general(a, b, _NT, preferred_element_type=jnp.float32)


def _rope(x, cos, sin_signed, first_half):
    swapped = jnp.where(first_half, pltpu.roll(x, LANES - HEAD_DIM // 2, 1),
                        pltpu.roll(x, HEAD_DIM // 2, 1))
    return x * cos + swapped * sin_signed


def _project(x, g_ref, w_ref, tab_refs, q_ref, k_ref, vt_ref, z_ref, n_kv, v_blk):
    tm = x.shape[0]
    kvw = n_kv * HEAD_DIM
    ms = jnp.mean(x * x, axis=-1, keepdims=True)
    hn = (x * lax.rsqrt(ms + NORM_EPS) * g_ref[...]).astype(jnp.bfloat16)

    lane = lax.broadcasted_iota(jnp.int32, (tm, LANES), 1)
    first_half = (lane % HEAD_DIM) < (HEAD_DIM // 2)
    cq, sq, ck, sk = (t[...] for t in tab_refs)

    for c in range(MIX_WIDTH // MXU_COLS):
        q = _dot(hn, w_ref[:, c * MXU_COLS:(c + 1) * MXU_COLS])
        for hh in range(MXU_COLS // LANES):
            lo = c * MXU_COLS + hh * LANES
            q_ref[0, :, lo:lo + LANES] = _rope(
                q[:, hh * LANES:(hh + 1) * LANES], cq, sq, first_half).astype(jnp.bfloat16)

    kv = _dot(hn, w_ref[:, MIX_WIDTH:MIX_WIDTH + 2 * kvw])
    for c in range(kvw // LANES):
        k = _rope(kv[:, c * LANES:(c + 1) * LANES], ck, sk, first_half).astype(jnp.bfloat16)
        k_ref[0, 2 * c] = k[:, :HEAD_DIM]
        k_ref[0, 2 * c + 1] = k[:, HEAD_DIM:]

    ones = jnp.ones((V_ROWS - HEAD_DIM, v_blk), jnp.bfloat16)
    for c in range(kvw // LANES):
        vt = kv[:, kvw + c * LANES:kvw + (c + 1) * LANES].T.astype(jnp.bfloat16)
        for hh in range(2):
            for b in range(tm // v_blk):
                vt_ref[0, 2 * c + hh, b, 0:HEAD_DIM, :] = vt[hh * HEAD_DIM:(hh + 1) * HEAD_DIM,
                                                              b * v_blk:(b + 1) * v_blk]
                vt_ref[0, 2 * c + hh, b, HEAD_DIM:V_ROWS, :] = ones

    z0 = MIX_WIDTH + 2 * kvw
    for c in range(MIX_WIDTH // MXU_COLS):
        z = _dot(hn, w_ref[:, z0 + c * MXU_COLS:z0 + (c + 1) * MXU_COLS])
        z_ref[0, :, c * MXU_COLS:(c + 1) * MXU_COLS] = (
            z * (1.0 / (1.0 + jnp.exp(-z)))).astype(jnp.bfloat16)


def _in_proj_kernel(n_kv, v_blk, h_ref, g_ref, w_ref, cq_ref, sq_ref, ck_ref, sk_ref,
                    q_ref, k_ref, vt_ref, z_ref):
    _project(h_ref[0], g_ref, w_ref, (cq_ref, sq_ref, ck_ref, sk_ref),
             q_ref, k_ref, vt_ref, z_ref, n_kv, v_blk)


def _mid_kernel(n_kv, v_blk, o_ref, sz_ref, h_ref, wo_ref, g_ref, w_ref,
                cq_ref, sq_ref, ck_ref, sk_ref, hout_ref, q_ref, k_ref, vt_ref, z_ref):
    h = h_ref[0] + _dot(o_ref[0] * sz_ref[0], wo_ref[...])
    hout_ref[0] = h
    _project(h, g_ref, w_ref, (cq_ref, sq_ref, ck_ref, sk_ref),
             q_ref, k_ref, vt_ref, z_ref, n_kv, v_blk)


def _proj_out_specs(B, S, tm, n_kv, v_blk):
    specs = [
        pl.BlockSpec((1, tm, MIX_WIDTH), lambda b, t: (b, t, 0)),
        pl.BlockSpec((1, n_kv, tm, HEAD_DIM), lambda b, t: (b, 0, t, 0)),
        pl.BlockSpec((1, n_kv, tm // v_blk, V_ROWS, v_blk), lambda b, t: (b, 0, t, 0, 0)),
        pl.BlockSpec((1, tm, MIX_WIDTH), lambda b, t: (b, t, 0)),
    ]
    shapes = [
        jax.ShapeDtypeStruct((B, S, MIX_WIDTH), jnp.bfloat16),
        jax.ShapeDtypeStruct((B, n_kv, S, HEAD_DIM), jnp.bfloat16),
        jax.ShapeDtypeStruct((B, n_kv, S // v_blk, V_ROWS, v_blk), jnp.bfloat16),
        jax.ShapeDtypeStruct((B, S, MIX_WIDTH), jnp.bfloat16),
    ]
    return specs, shapes


def _resident(shape):
    return pl.BlockSpec(shape, lambda b, t: (0,) * len(shape), pipeline_mode=pl.Buffered(1))


def _in_proj(h, g, w, tabs, n_kv, v_blk):
    B, S, _ = h.shape
    tm = TOKEN_TILE
    tab_spec = pl.BlockSpec((tm, LANES), lambda b, t: (t, 0))
    out_specs, out_shape = _proj_out_specs(B, S, tm, n_kv, v_blk)
    return pl.pallas_call(
        functools.partial(_in_proj_kernel, n_kv, v_blk),
        grid=(B, S // tm),
        in_specs=[
            pl.BlockSpec((1, tm, D_MODEL), lambda b, t: (b, t, 0)),
            _resident((1, D_MODEL)),
            _resident(w.shape),
            tab_spec, tab_spec, tab_spec, tab_spec,
        ],
        out_specs=out_specs,
        out_shape=out_shape,
        compiler_params=pltpu.CompilerParams(
            dimension_semantics=("parallel", "parallel"), vmem_limit_bytes=VMEM_LIMIT),
        name="in_proj",
    )(h, g, w, *tabs)


def _mid(o, sz, h, w_o, g, w, tabs, n_kv, v_blk):
    B, S, _ = h.shape
    tm = TOKEN_TILE
    row_spec = pl.BlockSpec((1, tm, D_MODEL), lambda b, t: (b, t, 0))
    tab_spec = pl.BlockSpec((tm, LANES), lambda b, t: (t, 0))
    out_specs, out_shape = _proj_out_specs(B, S, tm, n_kv, v_blk)
    return pl.pallas_call(
        functools.partial(_mid_kernel, n_kv, v_blk),
        grid=(B, S // tm),
        in_specs=[
            row_spec, row_spec, row_spec,
            _resident(w_o.shape),
            _resident((1, D_MODEL)),
            _resident(w.shape),
            tab_spec, tab_spec, tab_spec, tab_spec,
        ],
        out_specs=[row_spec] + out_specs,
        out_shape=[jax.ShapeDtypeStruct((B, S, D_MODEL), jnp.float32)] + out_shape,
        compiler_params=pltpu.CompilerParams(
            dimension_semantics=("parallel", "parallel"), vmem_limit_bytes=VMEM_LIMIT),
        name="mid",
    )(o, sz, h, w_o, g, w, *tabs)


def _out_proj_kernel(o_ref, z_ref, h_ref, w_ref, g_ref, out_ref):
    h = h_ref[0] + _dot(o_ref[0] * z_ref[0], w_ref[...])
    ms = jnp.mean(h * h, axis=-1, keepdims=True)
    out_ref[0] = h * lax.rsqrt(ms + NORM_EPS) * g_ref[...]


def _out_proj(o, sz, h, w, g):
    B, S, _ = h.shape
    tm = TOKEN_TILE
    row_spec = pl.BlockSpec((1, tm, D_MODEL), lambda b, t: (b, t, 0))
    return pl.pallas_call(
        _out_proj_kernel,
        grid=(B, S // tm),
        in_specs=[row_spec, row_spec, row_spec, _resident(w.shape), _resident((1, D_MODEL))],
        out_specs=row_spec,
        out_shape=jax.ShapeDtypeStruct((B, S, D_MODEL), jnp.float32),
        compiler_params=pltpu.CompilerParams(
            dimension_semantics=("parallel", "parallel"), vmem_limit_bytes=VMEM_LIMIT),
        name="out_proj",
    )(o, sz, h, w, g)


def _stack_heads(q, n):
    return jnp.concatenate([q[:, g * HEAD_DIM:(g + 1) * HEAD_DIM] for g in range(n)], axis=0)


def _unstack_heads_t(ot, n, rows):
    st = jnp.concatenate([ot[:, g * rows:(g + 1) * rows] for g in range(n)], axis=0)
    return st.T


def _swa_kernel(q_ref, k_ref, vt_ref, sink_ref, mask_ref, o_ref, s_ref, p_ref):
    n = pl.program_id(1)
    n_kv = k_ref.shape[1]
    gw = q_ref.shape[2] // n_kv
    group = gw // HEAD_DIM
    n_sub = q_ref.shape[1] // WINDOW
    units = [(c, sb) for sb in range(n_sub) for c in range(n_kv)]
    mx, m = {}, {}

    def band(sb):
        qb = n * n_sub + sb
        return qb, jnp.maximum(qb - 1, 0)

    def scores(u):
        c, sb = units[u]
        qb, first = band(sb)
        qs = _stack_heads(q_ref[0, sb * WINDOW:(sb + 1) * WINDOW, c * gw:(c + 1) * gw], group)
        kb = k_ref[0, c, pl.ds(pl.multiple_of(first * WINDOW, WINDOW), 2 * WINDOW), :]
        st = _dot_nt(kb, qs) + mask_ref[jnp.where(qb == 0, 1, 0)]
        s_ref[u % 3] = st
        mx[u] = jnp.max(st, axis=0, keepdims=True)

    def probs(u):
        c, _ = units[u]
        m[u] = jnp.maximum(mx[u], sink_ref[c])
        p_ref[u % 2] = jnp.exp2(s_ref[u % 3 + jnp.minimum(n, 0)] - m[u]).astype(jnp.bfloat16)

    def output(u):
        c, sb = units[u]
        _, first = band(sb)
        vt = jnp.concatenate([vt_ref[0, c, first], vt_ref[0, c, first + 1]], axis=1)
        acc = _dot(vt, p_ref[u % 2])
        denom = acc[HEAD_DIM:HEAD_DIM + 1] + jnp.exp2(sink_ref[c] - m[u])
        ot = acc[:HEAD_DIM] / denom
        o_ref[0, sb * WINDOW:(sb + 1) * WINDOW, c * gw:(c + 1) * gw] = _unstack_heads_t(
            ot, group, WINDOW).astype(jnp.bfloat16)

    for slot in range(len(units) + 2):
        if slot < len(units):
            scores(slot)
        if 0 <= slot - 1 < len(units):
            probs(slot - 1)
        if 0 <= slot - 2 < len(units):
            output(slot - 2)


def _swa_mask(rows):
    key = lax.broadcasted_iota(jnp.int32, (2 * WINDOW, rows), 0)
    qpos = lax.broadcasted_iota(jnp.int32, (2 * WINDOW, rows), 1) % WINDOW
    masks = []
    for off in (WINDOW, 0):
        dist = qpos + off - key
        masks.append(jnp.where((dist >= 0) & (dist < WINDOW), 0.0, NEG))
    return jnp.stack(masks).astype(jnp.float32)


def _swa(q, k, vt, sinks2):
    B, S, _ = q.shape
    n_kv = k.shape[1]
    rows = (MIX_WIDTH // n_kv // HEAD_DIM) * WINDOW
    tq = SWA_Q_TILE
    return pl.pallas_call(
        _swa_kernel,
        grid=(B, S // tq),
        in_specs=[
            pl.BlockSpec((1, tq, MIX_WIDTH), lambda b, n: (b, n, 0)),
            pl.BlockSpec((1, n_kv, S, HEAD_DIM), lambda b, n: (b, 0, 0, 0)),
            pl.BlockSpec((1, n_kv, S // WINDOW, V_ROWS, WINDOW), lambda b, n: (b, 0, 0, 0, 0)),
            pl.BlockSpec((n_kv, 1, rows), lambda b, n: (0, 0, 0)),
            pl.BlockSpec((2, 2 * WINDOW, rows), lambda b, n: (0, 0, 0)),
        ],
        out_specs=pl.BlockSpec((1, tq, MIX_WIDTH), lambda b, n: (b, n, 0)),
        out_shape=jax.ShapeDtypeStruct((B, S, MIX_WIDTH), jnp.bfloat16),
        scratch_shapes=[
            pltpu.VMEM((3, 2 * WINDOW, rows), jnp.float32),
            pltpu.VMEM((2, 2 * WINDOW, rows), jnp.bfloat16),
        ],
        compiler_params=pltpu.CompilerParams(
            dimension_semantics=("parallel", "arbitrary"),
            vmem_limit_bytes=VMEM_LIMIT),
        name="swa",
    )(q, k, vt, sinks2, _swa_mask(rows))


def _moba_kernel(q_ref, k_ref, vt_ref, cmask_ref, o_ref, km_ref, qs_ref, bias_ref,
                 s0_ref, s1_ref, mx0_ref, mx1_ref, m_ref, acc_ref):
    i = pl.program_id(1)
    n_kv = k_ref.shape[1]
    nb = k_ref.shape[2] // MOBA_BLOCK
    gw = q_ref.shape[2] // n_kv
    group = gw // HEAD_DIM
    rows = group * MOBA_BLOCK

    @pl.when(i == 0)
    def _():
        for c in range(n_kv):
            means = [jnp.sum(k_ref[0, c, j * MOBA_BLOCK:(j + 1) * MOBA_BLOCK, :].astype(jnp.float32),
                             axis=0, keepdims=True) for j in range(nb)]
            km = jnp.concatenate(means, axis=0) * (1.0 / MOBA_BLOCK)
            hi = km.astype(jnp.bfloat16)
            km_ref[c, 0:nb] = hi
            km_ref[c, nb:2 * nb] = (km - hi.astype(jnp.float32)).astype(jnp.bfloat16)

    bufs = ((s0_ref, mx0_ref), (s1_ref, mx1_ref))

    def keys(c, j):
        return k_ref[0, c, pl.ds(pl.multiple_of(j * MOBA_BLOCK, MOBA_BLOCK), MOBA_BLOCK), :]

    def scores(c, t):
        kj = k_ref[0, c, pl.ds(pl.multiple_of(t * MOBA_BLOCK, MOBA_BLOCK), 2 * MOBA_BLOCK), :]
        for g in range(group):
            cols = slice(g * MOBA_BLOCK, (g + 1) * MOBA_BLOCK)
            s = _dot_nt(kj, qs_ref[c, cols, :])
            for par, (s_ref, mx_ref) in enumerate(bufs):
                blk = s[par * MOBA_BLOCK:(par + 1) * MOBA_BLOCK]
                s_ref[c, :, cols] = blk
                mx_ref[c, :, cols] = jnp.max(blk, axis=0, keepdims=True)

    def score_tile(c, j, g, s_ref, mx_ref):
        cols = slice(g * MOBA_BLOCK, (g + 1) * MOBA_BLOCK)
        s = _dot_nt(keys(c, j), qs_ref[c, cols, :])
        s_ref[c, :, cols] = s
        mx_ref[c, :, cols] = jnp.max(s, axis=0, keepdims=True)

    def accumulate_tile(c, j, cols, p, alpha):
        acc_ref[c, :, cols] = acc_ref[c, :, cols] * alpha + _dot(vt_ref[0, c, j], p)

    def select_blocks(c):
        gt = _dot_nt(km_ref[c], qs_ref[c])
        blk = lax.broadcasted_iota(jnp.int32, (nb, rows), 0).astype(jnp.float32)
        past = blk < i.astype(jnp.float32)
        gate = jnp.where(past, gt[0:nb] + gt[nb:2 * nb], NEG)
        chosen = jnp.zeros((nb, rows), jnp.bool_)
        for _ in range(MOBA_TOPK):
            cur = jnp.where(chosen, BELOW_NEG, gate)
            best = jnp.max(cur, axis=0, keepdims=True)
            first = jnp.min(jnp.where(cur == best, blk, float(nb)), axis=0, keepdims=True)
            chosen = chosen | (blk == first)
        bias_ref[c] = jnp.where(chosen & past, 0.0, NEG)

    ds, dm, dp = {}, {}, {}

    def own_scores(c):
        ds[c] = _dot_nt(keys(c, i), qs_ref[c]) + cmask_ref[...]
        dm[c] = jnp.max(ds[c], axis=0, keepdims=True)

    def own_probs(c):
        m_ref[c] = dm[c]
        dp[c] = jnp.exp2(ds[c] - dm[c]).astype(jnp.bfloat16)

    def own_out(c):
        acc_ref[c] = _dot(vt_ref[0, c, i], dp[c])

    for c in range(n_kv):
        qs_ref[c] = _stack_heads(q_ref[0, :, c * gw:(c + 1) * gw], group)
    for slot in range(n_kv + 2):
        if slot < n_kv:
            own_scores(slot)
            select_blocks(slot)
            if slot < n_kv - 1:
                scores(slot, 0)
        if 0 <= slot - 1 < n_kv:
            own_probs(slot - 1)
        if 0 <= slot - 2 < n_kv:
            own_out(slot - 2)

    def pair(t, refill):
        pending = None
        for c in range(n_kv):
            ahead = (n_kv - 1, t) if c == 0 else ((c - 1, t + 2) if refill else None)
            m = m_ref[c]
            stats = []
            for par, (s_ref, mx_ref) in enumerate(bufs):
                bias = bias_ref[c, pl.ds(t + par, 1), :]
                m_new = jnp.maximum(m, mx_ref[c] + bias)
                stats.append((m_new - bias, jnp.exp2(m - m_new)))
                m = m_new
            m_ref[c] = m
            for g in range(group):
                cols = slice(g * MOBA_BLOCK, (g + 1) * MOBA_BLOCK)
                for par, (s_ref, mx_ref) in enumerate(bufs):
                    if ahead is not None:
                        score_tile(ahead[0], ahead[1] + par, g, s_ref, mx_ref)
                    shift, alpha = stats[par]
                    p = jnp.exp2(s_ref[c, :, cols] - shift[:, cols]).astype(jnp.bfloat16)
                    if pending is not None:
                        accumulate_tile(*pending)
                    pending = (c, t + par, cols, p, alpha[:, cols])
        accumulate_tile(*pending)

    n_pairs = (i + 1) // 2

    def steady(u, carry):
        pair(2 * u, True)
        return carry

    lax.fori_loop(0, n_pairs - 1, steady, 0)

    @pl.when(n_pairs > 0)
    def _():
        pair(2 * (n_pairs - 1), False)

    for c in range(n_kv):
        acc = acc_ref[c]
        ot = acc[:HEAD_DIM] / acc[HEAD_DIM:HEAD_DIM + 1]
        o_ref[0, :, c * gw:(c + 1) * gw] = _unstack_heads_t(ot, group, MOBA_BLOCK).astype(jnp.bfloat16)


def _moba_causal_mask(rows):
    key = lax.broadcasted_iota(jnp.int32, (MOBA_BLOCK, rows), 0)
    qpos = lax.broadcasted_iota(jnp.int32, (MOBA_BLOCK, rows), 1) % MOBA_BLOCK
    return jnp.where(key <= qpos, 0.0, NEG).astype(jnp.float32)


def _moba(q, k, vt):
    B, S, _ = q.shape
    n_kv = k.shape[1]
    nb = S // MOBA_BLOCK
    rows = (MIX_WIDTH // n_kv // HEAD_DIM) * MOBA_BLOCK
    return pl.pallas_call(
        _moba_kernel,
        grid=(B, nb),
        in_specs=[
            pl.BlockSpec((1, MOBA_BLOCK, MIX_WIDTH), lambda b, i: (b, i, 0)),
            pl.BlockSpec((1, n_kv, S, HEAD_DIM), lambda b, i: (b, 0, 0, 0)),
            pl.BlockSpec((1, n_kv, nb, V_ROWS, MOBA_BLOCK), lambda b, i: (b, 0, 0, 0, 0)),
            pl.BlockSpec((MOBA_BLOCK, rows), lambda b, i: (0, 0)),
        ],
        out_specs=pl.BlockSpec((1, MOBA_BLOCK, MIX_WIDTH), lambda b, i: (b, i, 0)),
        out_shape=jax.ShapeDtypeStruct((B, S, MIX_WIDTH), jnp.bfloat16),
        scratch_shapes=[
            pltpu.VMEM((n_kv, 2 * nb, HEAD_DIM), jnp.bfloat16),
            pltpu.VMEM((n_kv, rows, HEAD_DIM), jnp.bfloat16),
            pltpu.VMEM((n_kv, nb, rows), jnp.float32),
            pltpu.VMEM((n_kv, MOBA_BLOCK, rows), jnp.float32),
            pltpu.VMEM((n_kv, MOBA_BLOCK, rows), jnp.float32),
            pltpu.VMEM((n_kv, 1, rows), jnp.float32),
            pltpu.VMEM((n_kv, 1, rows), jnp.float32),
            pltpu.VMEM((n_kv, 1, rows), jnp.float32),
            pltpu.VMEM((n_kv, V_ROWS, rows), jnp.float32),
        ],
        compiler_params=pltpu.CompilerParams(
            dimension_semantics=("parallel", "arbitrary"),
            vmem_limit_bytes=VMEM_LIMIT),
        name="moba",
    )(q, k, vt, _moba_causal_mask(rows))


def _rope_tables(seq, q_scale):
    pos = jnp.arange(seq, dtype=jnp.float32)
    inv = ROPE_THETA ** (-jnp.arange(0, HEAD_DIM, 2, dtype=jnp.float32) / HEAD_DIM)
    ang = pos[:, None] * inv[None, :]
    cos, sin = jnp.cos(ang), jnp.sin(ang)
    ck = jnp.tile(cos, (1, LANES // (HEAD_DIM // 2)))
    sk = jnp.tile(jnp.concatenate([-sin, sin], axis=1), (1, LANES // HEAD_DIM))
    return ck * q_scale, sk * q_scale, ck, sk


def kernel(x, norm_g, w_in_a, sinks_a, w_in_b, w_out, final_g):
    B, S, _ = x.shape
    depth = norm_g.shape[0]
    assert S % TOKEN_TILE == 0 and S % SWA_Q_TILE == 0 and S >= 2 * MOBA_BLOCK
    tabs = _rope_tables(S, LOG2E / math.sqrt(HEAD_DIM))

    def layer_params(i):
        g = norm_g[i].reshape(1, D_MODEL)
        if i % 2 == 0:
            return g, w_in_a[i // 2].astype(jnp.bfloat16), N_KV_A, WINDOW
        return g, w_in_b[i // 2].astype(jnp.bfloat16), N_KV_B, MOBA_BLOCK

    h = x
    q, k, vt, sz = _in_proj(h, *layer_params(0)[:2], tabs, *layer_params(0)[2:])
    for i in range(depth):
        if i % 2 == 0:
            group = N_HEADS // N_KV_A
            sinks2 = jnp.repeat(sinks_a[i // 2].astype(jnp.float32) * LOG2E, WINDOW).reshape(
                N_KV_A, 1, group * WINDOW)
            o = _swa(q, k, vt, sinks2)
        else:
            o = _moba(q, k, vt)
        w_o = w_out[i].astype(jnp.bfloat16)
        if i + 1 < depth:
            g, w, n_kv, v_blk = layer_params(i + 1)
            h, q, k, vt, sz = _mid(o, sz, h, w_o, g, w, tabs, n_kv, v_blk)
        else:
            h = _out_proj(o, sz, h, w_o, final_g.reshape(1, D_MODEL))
    return h
```

```python
import functools
import math

import jax
import jax.numpy as jnp
from jax import lax
from jax.experimental import pallas as pl
from jax.experimental.pallas import tpu as pltpu

D_MODEL = 1024
HEAD_DIM = 64
N_HEADS = 16
MIX_WIDTH = N_HEADS * HEAD_DIM
N_KV_A = 2
N_KV_B = 4
WINDOW = 128
MOBA_BLOCK = 256
MOBA_TOPK = 3
ROPE_THETA = 10000.0
NORM_EPS = 1e-5
NEG = -1e30
BELOW_NEG = -3e38
LOG2E = 1.4426950408889634

LANES = 128
MXU_COLS = 256
V_ROWS = HEAD_DIM + 16
TOKEN_TILE = 512
ROW_PARTS = 2
SWA_Q_TILE = 1024
VMEM_LIMIT = 48 * 1024 * 1024

_NT = (((1,), (1,)), ((), ()))


def _dot(a, b):
    return jnp.dot(a, b, preferred_element_type=jnp.float32)


def _dot_nt(a, b):
    return lax.dot_general(a, b, _NT, preferred_element_type=jnp.float32)


def _rope(x, cos, sin_signed, first_half):
    swapped = jnp.where(first_half, pltpu.roll(x, LANES - HEAD_DIM // 2, 1),
                        pltpu.roll(x, HEAD_DIM // 2, 1))
    return x * cos + swapped * sin_signed


def _project(x, r0, g_ref, w_ref, tab_refs, q_ref, k_ref, vt_ref, z_ref, n_kv, v_blk):
    tr = x.shape[0]
    assert tr % v_blk == 0 and r0 % v_blk == 0
    rows = slice(r0, r0 + tr)
    kvw = n_kv * HEAD_DIM
    ms = jnp.mean(x * x, axis=-1, keepdims=True)
    hn = (x * lax.rsqrt(ms + NORM_EPS) * g_ref[...]).astype(jnp.bfloat16)

    lane = lax.broadcasted_iota(jnp.int32, (tr, LANES), 1)
    first_half = (lane % HEAD_DIM) < (HEAD_DIM // 2)
    cq, sq, ck, sk = (t[rows, :] for t in tab_refs)

    for c in range(MIX_WIDTH // MXU_COLS):
        q = _dot(hn, w_ref[:, c * MXU_COLS:(c + 1) * MXU_COLS])
        for hh in range(MXU_COLS // LANES):
            lo = c * MXU_COLS + hh * LANES
            q_ref[0, rows, lo:lo + LANES] = _rope(
                q[:, hh * LANES:(hh + 1) * LANES], cq, sq, first_half).astype(jnp.bfloat16)

    kv = _dot(hn, w_ref[:, MIX_WIDTH:MIX_WIDTH + 2 * kvw])
    for c in range(kvw // LANES):
        k = _rope(kv[:, c * LANES:(c + 1) * LANES], ck, sk, first_half).astype(jnp.bfloat16)
        k_ref[0, 2 * c, rows, :] = k[:, :HEAD_DIM]
        k_ref[0, 2 * c + 1, rows, :] = k[:, HEAD_DIM:]

    ones = jnp.ones((V_ROWS - HEAD_DIM, v_blk), jnp.bfloat16)
    for c in range(kvw // LANES):
        vt = kv[:, kvw + c * LANES:kvw + (c + 1) * LANES].T.astype(jnp.bfloat16)
        for hh in range(2):
            for b in range(tr // v_blk):
                blk = r0 // v_blk + b
                vt_ref[0, 2 * c + hh, blk, 0:HEAD_DIM, :] = vt[hh * HEAD_DIM:(hh + 1) * HEAD_DIM,
                                                                b * v_blk:(b + 1) * v_blk]
                vt_ref[0, 2 * c + hh, blk, HEAD_DIM:V_ROWS, :] = ones

    z0 = MIX_WIDTH + 2 * kvw
    for c in range(MIX_WIDTH // MXU_COLS):
        z = _dot(hn, w_ref[:, z0 + c * MXU_COLS:z0 + (c + 1) * MXU_COLS])
        z_ref[0, rows, c * MXU_COLS:(c + 1) * MXU_COLS] = (
            z * (1.0 / (1.0 + jnp.exp(-z)))).astype(jnp.bfloat16)


def _row_parts(tm):
    return [(r, tm // ROW_PARTS) for r in range(0, tm, tm // ROW_PARTS)]


def _in_proj_kernel(n_kv, v_blk, h_ref, g_ref, w_ref, cq_ref, sq_ref, ck_ref, sk_ref,
                    q_ref, k_ref, vt_ref, z_ref):
    for r0, tr in _row_parts(h_ref.shape[1]):
        _project(h_ref[0, r0:r0 + tr, :], r0, g_ref, w_ref, (cq_ref, sq_ref, ck_ref, sk_ref),
                 q_ref, k_ref, vt_ref, z_ref, n_kv, v_blk)


def _mid_kernel(n_kv, v_blk, o_ref, sz_ref, h_ref, wo_ref, g_ref, w_ref,
                cq_ref, sq_ref, ck_ref, sk_ref, hout_ref, q_ref, k_ref, vt_ref, z_ref):
    parts = _row_parts(h_ref.shape[1])
    hs = []
    for r0, tr in parts:
        rows = slice(r0, r0 + tr)
        h = h_ref[0, rows, :] + _dot(o_ref[0, rows, :] * sz_ref[0, rows, :], wo_ref[...])
        hout_ref[0, rows, :] = h
        hs.append(h)
    for (r0, tr), h in zip(parts, hs):
        _project(h, r0, g_ref, w_ref, (cq_ref, sq_ref, ck_ref, sk_ref),
                 q_ref, k_ref, vt_ref, z_ref, n_kv, v_blk)


def _proj_out_specs(B, S, tm, n_kv, v_blk):
    specs = [
        pl.BlockSpec((1, tm, MIX_WIDTH), lambda b, t: (b, t, 0)),
        pl.BlockSpec((1, n_kv, tm, HEAD_DIM), lambda b, t: (b, 0, t, 0)),
        pl.BlockSpec((1, n_kv, tm // v_blk, V_ROWS, v_blk), lambda b, t: (b, 0, t, 0, 0)),
        pl.BlockSpec((1, tm, MIX_WIDTH), lambda b, t: (b, t, 0)),
    ]
    shapes = [
        jax.ShapeDtypeStruct((B, S, MIX_WIDTH), jnp.bfloat16),
        jax.ShapeDtypeStruct((B, n_kv, S, HEAD_DIM), jnp.bfloat16),
        jax.ShapeDtypeStruct((B, n_kv, S // v_blk, V_ROWS, v_blk), jnp.bfloat16),
        jax.ShapeDtypeStruct((B, S, MIX_WIDTH), jnp.bfloat16),
    ]
    return specs, shapes


def _resident(shape):
    return pl.BlockSpec(shape, lambda b, t: (0,) * len(shape), pipeline_mode=pl.Buffered(1))


def _in_proj(h, g, w, tabs, n_kv, v_blk):
    B, S, _ = h.shape
    tm = TOKEN_TILE
    tab_spec = pl.BlockSpec((tm, LANES), lambda b, t: (t, 0))
    out_specs, out_shape = _proj_out_specs(B, S, tm, n_kv, v_blk)
    return pl.pallas_call(
        functools.partial(_in_proj_kernel, n_kv, v_blk),
        grid=(B, S // tm),
        in_specs=[
            pl.BlockSpec((1, tm, D_MODEL), lambda b, t: (b, t, 0)),
            _resident((1, D_MODEL)),
            _resident(w.shape),
            tab_spec, tab_spec, tab_spec, tab_spec,
        ],
        out_specs=out_specs,
        out_shape=out_shape,
        compiler_params=pltpu.CompilerParams(
            dimension_semantics=("parallel", "parallel"), vmem_limit_bytes=VMEM_LIMIT),
        name="in_proj",
    )(h, g, w, *tabs)


def _mid(o, sz, h, w_o, g, w, tabs, n_kv, v_blk):
    B, S, _ = h.shape
    tm = TOKEN_TILE
    row_spec = pl.BlockSpec((1, tm, D_MODEL), lambda b, t: (b, t, 0))
    tab_spec = pl.BlockSpec((tm, LANES), lambda b, t: (t, 0))
    out_specs, out_shape = _proj_out_specs(B, S, tm, n_kv, v_blk)
    return pl.pallas_call(
        functools.partial(_mid_kernel, n_kv, v_blk),
        grid=(B, S // tm),
        in_specs=[
            row_spec, row_spec, row_spec,
            _resident(w_o.shape),
            _resident((1, D_MODEL)),
            _resident(w.shape),
            tab_spec, tab_spec, tab_spec, tab_spec,
        ],
        out_specs=[row_spec] + out_specs,
        out_shape=[jax.ShapeDtypeStruct((B, S, D_MODEL), jnp.float32)] + out_shape,
        compiler_params=pltpu.CompilerParams(
            dimension_semantics=("parallel", "parallel"), vmem_limit_bytes=VMEM_LIMIT),
        name="mid",
    )(o, sz, h, w_o, g, w, *tabs)


def _out_proj_kernel(o_ref, z_ref, h_ref, w_ref, g_ref, out_ref):
    h = h_ref[0] + _dot(o_ref[0] * z_ref[0], w_ref[...])
    ms = jnp.mean(h * h, axis=-1, keepdims=True)
    out_ref[0] = h * lax.rsqrt(ms + NORM_EPS) * g_ref[...]


def _out_proj(o, sz, h, w, g):
    B, S, _ = h.shape
    tm = TOKEN_TILE
    row_spec = pl.BlockSpec((1, tm, D_MODEL), lambda b, t: (b, t, 0))
    return pl.pallas_call(
        _out_proj_kernel,
        grid=(B, S // tm),
        in_specs=[row_spec, row_spec, row_spec, _resident(w.shape), _resident((1, D_MODEL))],
        out_specs=row_spec,
        out_shape=jax.ShapeDtypeStruct((B, S, D_MODEL), jnp.float32),
        compiler_params=pltpu.CompilerParams(
            dimension_semantics=("parallel", "parallel"), vmem_limit_bytes=VMEM_LIMIT),
        name="out_proj",
    )(o, sz, h, w, g)


def _stack_heads(q, n):
    return jnp.concatenate([q[:, g * HEAD_DIM:(g + 1) * HEAD_DIM] for g in range(n)], axis=0)


def _unstack_heads_t(ot, n, rows):
    st = jnp.concatenate([ot[:, g * rows:(g + 1) * rows] for g in range(n)], axis=0)
    return st.T


def _swa_kernel(q_ref, k_ref, vt_ref, sink_ref, mask_ref, o_ref, s_ref, p_ref):
    n = pl.program_id(1)
    n_kv = k_ref.shape[1]
    gw = q_ref.shape[2] // n_kv
    group = gw // HEAD_DIM
    n_sub = q_ref.shape[1] // WINDOW
    units = [(c, sb) for sb in range(n_sub) for c in range(n_kv)]
    mx, m = {}, {}

    def band(sb):
        qb = n * n_sub + sb
        return qb, jnp.maximum(qb - 1, 0)

    def scores(u):
        c, sb = units[u]
        qb, first = band(sb)
        qs = _stack_heads(q_ref[0, sb * WINDOW:(sb + 1) * WINDOW, c * gw:(c + 1) * gw], group)
        kb = k_ref[0, c, pl.ds(pl.multiple_of(first * WINDOW, WINDOW), 2 * WINDOW), :]
        st = _dot_nt(kb, qs) + mask_ref[jnp.where(qb == 0, 1, 0)]
        s_ref[u % 3] = st
        mx[u] = jnp.max(st, axis=0, keepdims=True)

    def probs(u):
        c, _ = units[u]
        m[u] = jnp.maximum(mx[u], sink_ref[c])
        p_ref[u % 2] = jnp.exp2(s_ref[u % 3 + jnp.minimum(n, 0)] - m[u]).astype(jnp.bfloat16)

    def output(u):
        c, sb = units[u]
        _, first = band(sb)
        vt = jnp.concatenate([vt_ref[0, c, first], vt_ref[0, c, first + 1]], axis=1)
        acc = _dot(vt, p_ref[u % 2])
        denom = acc[HEAD_DIM:HEAD_DIM + 1] + jnp.exp2(sink_ref[c] - m[u])
        ot = acc[:HEAD_DIM] / denom
        o_ref[0, sb * WINDOW:(sb + 1) * WINDOW, c * gw:(c + 1) * gw] = _unstack_heads_t(
            ot, group, WINDOW).astype(jnp.bfloat16)

    for slot in range(len(units) + 2):
        if slot < len(units):
            scores(slot)
        if 0 <= slot - 1 < len(units):
            probs(slot - 1)
        if 0 <= slot - 2 < len(units):
            output(slot - 2)


def _swa_mask(rows):
    key = lax.broadcasted_iota(jnp.int32, (2 * WINDOW, rows), 0)
    qpos = lax.broadcasted_iota(jnp.int32, (2 * WINDOW, rows), 1) % WINDOW
    masks = []
    for off in (WINDOW, 0):
        dist = qpos + off - key
        masks.append(jnp.where((dist >= 0) & (dist < WINDOW), 0.0, NEG))
    return jnp.stack(masks).astype(jnp.float32)


def _swa(q, k, vt, sinks2):
    B, S, _ = q.shape
    n_kv = k.shape[1]
    rows = (MIX_WIDTH // n_kv // HEAD_DIM) * WINDOW
    tq = SWA_Q_TILE
    return pl.pallas_call(
        _swa_kernel,
        grid=(B, S // tq),
        in_specs=[
            pl.BlockSpec((1, tq, MIX_WIDTH), lambda b, n: (b, n, 0)),
            pl.BlockSpec((1, n_kv, S, HEAD_DIM), lambda b, n: (b, 0, 0, 0)),
            pl.BlockSpec((1, n_kv, S // WINDOW, V_ROWS, WINDOW), lambda b, n: (b, 0, 0, 0, 0)),
            pl.BlockSpec((n_kv, 1, rows), lambda b, n: (0, 0, 0)),
            pl.BlockSpec((2, 2 * WINDOW, rows), lambda b, n: (0, 0, 0)),
        ],
        out_specs=pl.BlockSpec((1, tq, MIX_WIDTH), lambda b, n: (b, n, 0)),
        out_shape=jax.ShapeDtypeStruct((B, S, MIX_WIDTH), jnp.bfloat16),
        scratch_shapes=[
            pltpu.VMEM((3, 2 * WINDOW, rows), jnp.float32),
            pltpu.VMEM((2, 2 * WINDOW, rows), jnp.bfloat16),
        ],
        compiler_params=pltpu.CompilerParams(
            dimension_semantics=("parallel", "arbitrary"),
            vmem_limit_bytes=VMEM_LIMIT),
        name="swa",
    )(q, k, vt, sinks2, _swa_mask(rows))


def _moba_kernel(q_ref, k_ref, vt_ref, cmask_ref, o_ref, km_ref, qs_ref, bias_ref,
                 s0_ref, s1_ref, mx0_ref, mx1_ref, m_ref, acc_ref):
    i = pl.program_id(1)
    n_kv = k_ref.shape[1]
    nb = k_ref.shape[2] // MOBA_BLOCK
    gw = q_ref.shape[2] // n_kv
    group = gw // HEAD_DIM
    rows = group * MOBA_BLOCK

    @pl.when(i == 0)
    def _():
        for c in range(n_kv):
            means = [jnp.sum(k_ref[0, c, j * MOBA_BLOCK:(j + 1) * MOBA_BLOCK, :].astype(jnp.float32),
                             axis=0, keepdims=True) for j in range(nb)]
            km = jnp.concatenate(means, axis=0) * (1.0 / MOBA_BLOCK)
            hi = km.astype(jnp.bfloat16)
            km_ref[c, 0:nb] = hi
            km_ref[c, nb:2 * nb] = (km - hi.astype(jnp.float32)).astype(jnp.bfloat16)

    bufs = ((s0_ref, mx0_ref), (s1_ref, mx1_ref))

    def keys(c, j):
        return k_ref[0, c, pl.ds(pl.multiple_of(j * MOBA_BLOCK, MOBA_BLOCK), MOBA_BLOCK), :]

    def scores(c, t):
        kj = k_ref[0, c, pl.ds(pl.multiple_of(t * MOBA_BLOCK, MOBA_BLOCK), 2 * MOBA_BLOCK), :]
        for g in range(group):
            cols = slice(g * MOBA_BLOCK, (g + 1) * MOBA_BLOCK)
            s = _dot_nt(kj, qs_ref[c, cols, :])
            for par, (s_ref, mx_ref) in enumerate(bufs):
                blk = s[par * MOBA_BLOCK:(par + 1) * MOBA_BLOCK]
                s_ref[c, :, cols] = blk
                mx_ref[c, :, cols] = jnp.max(blk, axis=0, keepdims=True)

    def score_tile(c, j, g, s_ref, mx_ref):
        cols = slice(g * MOBA_BLOCK, (g + 1) * MOBA_BLOCK)
        s = _dot_nt(keys(c, j), qs_ref[c, cols, :])
        s_ref[c, :, cols] = s
        mx_ref[c, :, cols] = jnp.max(s, axis=0, keepdims=True)

    def accumulate_tile(c, j, cols, p, alpha):
        acc_ref[c, :, cols] = acc_ref[c, :, cols] * alpha + _dot(vt_ref[0, c, j], p)

    def select_blocks(c):
        gt = _dot_nt(km_ref[c], qs_ref[c])
        blk = lax.broadcasted_iota(jnp.int32, (nb, rows), 0).astype(jnp.float32)
        past = blk < i.astype(jnp.float32)
        gate = jnp.where(past, gt[0:nb] + gt[nb:2 * nb], NEG)
        chosen = jnp.zeros((nb, rows), jnp.bool_)
        for _ in range(MOBA_TOPK):
            cur = jnp.where(chosen, BELOW_NEG, gate)
            best = jnp.max(cur, axis=0, keepdims=True)
            first = jnp.min(jnp.where(cur == best, blk, float(nb)), axis=0, keepdims=True)
            chosen = chosen | (blk == first)
        bias_ref[c] = jnp.where(chosen & past, 0.0, NEG)

    ds, dm, dp = {}, {}, {}

    def own_scores(c):
        ds[c] = _dot_nt(keys(c, i), qs_ref[c]) + cmask_ref[...]
        dm[c] = jnp.max(ds[c], axis=0, keepdims=True)

    def own_probs(c):
        m_ref[c] = dm[c]
        dp[c] = jnp.exp2(ds[c] - dm[c]).astype(jnp.bfloat16)

    def own_out(c):
        acc_ref[c] = _dot(vt_ref[0, c, i], dp[c])

    for c in range(n_kv):
        qs_ref[c] = _stack_heads(q_ref[0, :, c * gw:(c + 1) * gw], group)
    for slot in range(n_kv + 2):
        if slot < n_kv:
            own_scores(slot)
            select_blocks(slot)
            if slot < n_kv - 1:
                scores(slot, 0)
        if 0 <= slot - 1 < n_kv:
            own_probs(slot - 1)
        if 0 <= slot - 2 < n_kv:
            own_out(slot - 2)

    def pair(t, refill):
        pending = None
        for c in range(n_kv):
            ahead = (n_kv - 1, t) if c == 0 else ((c - 1, t + 2) if refill else None)
            m = m_ref[c]
            stats = []
            for par, (s_ref, mx_ref) in enumerate(bufs):
                bias = bias_ref[c, pl.ds(t + par, 1), :]
                m_new = jnp.maximum(m, mx_ref[c] + bias)
                stats.append((m_new - bias, jnp.exp2(m - m_new)))
                m = m_new
            m_ref[c] = m
            for g in range(group):
                cols = slice(g * MOBA_BLOCK, (g + 1) * MOBA_BLOCK)
                for par, (s_ref, mx_ref) in enumerate(bufs):
                    if ahead is not None:
                        score_tile(ahead[0], ahead[1] + par, g, s_ref, mx_ref)
                    shift, alpha = stats[par]
                    p = jnp.exp2(s_ref[c, :, cols] - shift[:, cols]).astype(jnp.bfloat16)
                    if pending is not None:
                        accumulate_tile(*pending)
                    pending = (c, t + par, cols, p, alpha[:, cols])
        accumulate_tile(*pending)

    n_pairs = (i + 1) // 2

    def steady(u, carry):
        pair(2 * u, True)
        return carry

    lax.fori_loop(0, n_pairs - 1, steady, 0)

    @pl.when(n_pairs > 0)
    def _():
        pair(2 * (n_pairs - 1), False)

    for c in range(n_kv):
        acc = acc_ref[c]
        ot = acc[:HEAD_DIM] / acc[HEAD_DIM:HEAD_DIM + 1]
        o_ref[0, :, c * gw:(c + 1) * gw] = _unstack_heads_t(ot, group, MOBA_BLOCK).astype(jnp.bfloat16)


def _moba_causal_mask(rows):
    key = lax.broadcasted_iota(jnp.int32, (MOBA_BLOCK, rows), 0)
    qpos = lax.broadcasted_iota(jnp.int32, (MOBA_BLOCK, rows), 1) % MOBA_BLOCK
    return jnp.where(key <= qpos, 0.0, NEG).astype(jnp.float32)


def _moba(q, k, vt):
    B, S, _ = q.shape
    n_kv = k.shape[1]
    nb = S // MOBA_BLOCK
    rows = (MIX_WIDTH // n_kv // HEAD_DIM) * MOBA_BLOCK
    return pl.pallas_call(
        _moba_kernel,
        grid=(B, nb),
        in_specs=[
            pl.BlockSpec((1, MOBA_BLOCK, MIX_WIDTH), lambda b, i: (b, i, 0)),
            pl.BlockSpec((1, n_kv, S, HEAD_DIM), lambda b, i: (b, 0, 0, 0)),
            pl.BlockSpec((1, n_kv, nb, V_ROWS, MOBA_BLOCK), lambda b, i: (b, 0, 0, 0, 0)),
            pl.BlockSpec((MOBA_BLOCK, rows), lambda b, i: (0, 0)),
        ],
        out_specs=pl.BlockSpec((1, MOBA_BLOCK, MIX_WIDTH), lambda b, i: (b, i, 0)),
        out_shape=jax.ShapeDtypeStruct((B, S, MIX_WIDTH), jnp.bfloat16),
        scratch_shapes=[
            pltpu.VMEM((n_kv, 2 * nb, HEAD_DIM), jnp.bfloat16),
            pltpu.VMEM((n_kv, rows, HEAD_DIM), jnp.bfloat16),
            pltpu.VMEM((n_kv, nb, rows), jnp.float32),
            pltpu.VMEM((n_kv, MOBA_BLOCK, rows), jnp.float32),
            pltpu.VMEM((n_kv, MOBA_BLOCK, rows), jnp.float32),
            pltpu.VMEM((n_kv, 1, rows), jnp.float32),
            pltpu.VMEM((n_kv, 1, rows), jnp.float32),
            pltpu.VMEM((n_kv, 1, rows), jnp.float32),
            pltpu.VMEM((n_kv, V_ROWS, rows), jnp.float32),
        ],
        compiler_params=pltpu.CompilerParams(
            dimension_semantics=("parallel", "arbitrary"),
            vmem_limit_bytes=VMEM_LIMIT),
        name="moba",
    )(q, k, vt, _moba_causal_mask(rows))


def _rope_tables(seq, q_scale):
    pos = jnp.arange(seq, dtype=jnp.float32)
    inv = ROPE_THETA ** (-jnp.arange(0, HEAD_DIM, 2, dtype=jnp.float32) / HEAD_DIM)
    ang = pos[:, None] * inv[None, :]
    cos, sin = jnp.cos(ang), jnp.sin(ang)
    ck = jnp.tile(cos, (1, LANES // (HEAD_DIM // 2)))
    sk = jnp.tile(jnp.concatenate([-sin, sin], axis=1), (1, LANES // HEAD_DIM))
    return ck * q_scale, sk * q_scale, ck, sk


def kernel(x, norm_g, w_in_a, sinks_a, w_in_b, w_out, final_g):
    B, S, _ = x.shape
    depth = norm_g.shape[0]
    assert S % TOKEN_TILE == 0 and S % SWA_Q_TILE == 0 and S >= 2 * MOBA_BLOCK
    tabs = _rope_tables(S, LOG2E / math.sqrt(HEAD_DIM))

    def layer_params(i):
        g = norm_g[i].reshape(1, D_MODEL)
        if i % 2 == 0:
            return g, w_in_a[i // 2].astype(jnp.bfloat16), N_KV_A, WINDOW
        return g, w_in_b[i // 2].astype(jnp.bfloat16), N_KV_B, MOBA_BLOCK

    h = x
    q, k, vt, sz = _in_proj(h, *layer_params(0)[:2], tabs, *layer_params(0)[2:])
    for i in range(depth):
        if i % 2 == 0:
            group = N_HEADS // N_KV_A
            sinks2 = jnp.repeat(sinks_a[i // 2].astype(jnp.float32) * LOG2E, WINDOW).reshape(
                N_KV_A, 1, group * WINDOW)
            o = _swa(q, k, vt, sinks2)
        else:
            o = _moba(q, k, vt)
        w_o = w_out[i].astype(jnp.bfloat16)
        if i + 1 < depth:
            g, w, n_kv, v_blk = layer_params(i + 1)
            h, q, k, vt, sz = _mid(o, sz, h, w_o, g, w, tabs, n_kv, v_blk)
        else:
            h = _out_proj(o, sz, h, w_o, final_g.reshape(1, D_MODEL))
    return h
```

```python
import functools
import math

import jax
import jax.numpy as jnp
from jax import lax
from jax.experimental import pallas as pl
from jax.experimental.pallas import tpu as pltpu

D_MODEL = 1024
HEAD_DIM = 64
N_HEADS = 16
MIX_WIDTH = N_HEADS * HEAD_DIM
N_KV_A = 2
N_KV_B = 4
WINDOW = 128
MOBA_BLOCK = 256
MOBA_TOPK = 3
ROPE_THETA = 10000.0
NORM_EPS = 1e-5
NEG = -1e30
BELOW_NEG = -3e38
LOG2E = 1.4426950408889634

LANES = 128
MXU_COLS = 256
V_ROWS = HEAD_DIM + 16
TOKEN_TILE = 512
ROW_PARTS = 2
SWA_Q_TILE = 2048
VMEM_LIMIT = 48 * 1024 * 1024

_NT = (((1,), (1,)), ((), ()))


def _dot(a, b):
    return jnp.dot(a, b, preferred_element_type=jnp.float32)


def _dot_nt(a, b):
    return lax.dot_general(a, b, _NT, preferred_element_type=jnp.float32)


def _rope(x, cos, sin_signed, first_half):
    swapped = jnp.where(first_half, pltpu.roll(x, LANES - HEAD_DIM // 2, 1),
                        pltpu.roll(x, HEAD_DIM // 2, 1))
    return x * cos + swapped * sin_signed


def _project(x, r0, g_ref, w_ref, tab_refs, q_ref, k_ref, vt_ref, z_ref, n_kv, v_blk):
    tr = x.shape[0]
    assert tr % v_blk == 0 and r0 % v_blk == 0
    rows = slice(r0, r0 + tr)
    kvw = n_kv * HEAD_DIM
    ms = jnp.mean(x * x, axis=-1, keepdims=True)
    hn = (x * lax.rsqrt(ms + NORM_EPS) * g_ref[...]).astype(jnp.bfloat16)

    lane = lax.broadcasted_iota(jnp.int32, (tr, LANES), 1)
    first_half = (lane % HEAD_DIM) < (HEAD_DIM // 2)
    cq, sq, ck, sk = (t[rows, :] for t in tab_refs)

    for c in range(MIX_WIDTH // MXU_COLS):
        q = _dot(hn, w_ref[:, c * MXU_COLS:(c + 1) * MXU_COLS])
        for hh in range(MXU_COLS // LANES):
            lo = c * MXU_COLS + hh * LANES
            q_ref[0, rows, lo:lo + LANES] = _rope(
                q[:, hh * LANES:(hh + 1) * LANES], cq, sq, first_half).astype(jnp.bfloat16)

    kv = _dot(hn, w_ref[:, MIX_WIDTH:MIX_WIDTH + 2 * kvw])
    for c in range(kvw // LANES):
        k = _rope(kv[:, c * LANES:(c + 1) * LANES], ck, sk, first_half).astype(jnp.bfloat16)
        k_ref[0, 2 * c, rows, :] = k[:, :HEAD_DIM]
        k_ref[0, 2 * c + 1, rows, :] = k[:, HEAD_DIM:]

    ones = jnp.ones((V_ROWS - HEAD_DIM, v_blk), jnp.bfloat16)
    for c in range(kvw // LANES):
        vt = kv[:, kvw + c * LANES:kvw + (c + 1) * LANES].T.astype(jnp.bfloat16)
        for hh in range(2):
            for b in range(tr // v_blk):
                blk = r0 // v_blk + b
                vt_ref[0, 2 * c + hh, blk, 0:HEAD_DIM, :] = vt[hh * HEAD_DIM:(hh + 1) * HEAD_DIM,
                                                                b * v_blk:(b + 1) * v_blk]
                vt_ref[0, 2 * c + hh, blk, HEAD_DIM:V_ROWS, :] = ones

    z0 = MIX_WIDTH + 2 * kvw
    for c in range(MIX_WIDTH // MXU_COLS):
        z = _dot(hn, w_ref[:, z0 + c * MXU_COLS:z0 + (c + 1) * MXU_COLS])
        z_ref[0, rows, c * MXU_COLS:(c + 1) * MXU_COLS] = (
            z * (1.0 / (1.0 + jnp.exp(-z)))).astype(jnp.bfloat16)


def _row_parts(tm):
    return [(r, tm // ROW_PARTS) for r in range(0, tm, tm // ROW_PARTS)]


def _in_proj_kernel(n_kv, v_blk, h_ref, g_ref, w_ref, cq_ref, sq_ref, ck_ref, sk_ref,
                    q_ref, k_ref, vt_ref, z_ref):
    for r0, tr in _row_parts(h_ref.shape[1]):
        _project(h_ref[0, r0:r0 + tr, :], r0, g_ref, w_ref, (cq_ref, sq_ref, ck_ref, sk_ref),
                 q_ref, k_ref, vt_ref, z_ref, n_kv, v_blk)


def _mid_kernel(n_kv, v_blk, o_ref, sz_ref, h_ref, wo_ref, g_ref, w_ref,
                cq_ref, sq_ref, ck_ref, sk_ref, hout_ref, q_ref, k_ref, vt_ref, z_ref):
    parts = _row_parts(h_ref.shape[1])
    hs = []
    for r0, tr in parts:
        rows = slice(r0, r0 + tr)
        h = h_ref[0, rows, :] + _dot(o_ref[0, rows, :] * sz_ref[0, rows, :], wo_ref[...])
        hout_ref[0, rows, :] = h
        hs.append(h)
    for (r0, tr), h in zip(parts, hs):
        _project(h, r0, g_ref, w_ref, (cq_ref, sq_ref, ck_ref, sk_ref),
                 q_ref, k_ref, vt_ref, z_ref, n_kv, v_blk)


def _proj_out_specs(B, S, tm, n_kv, v_blk):
    specs = [
        pl.BlockSpec((1, tm, MIX_WIDTH), lambda b, t: (b, t, 0)),
        pl.BlockSpec((1, n_kv, tm, HEAD_DIM), lambda b, t: (b, 0, t, 0)),
        pl.BlockSpec((1, n_kv, tm // v_blk, V_ROWS, v_blk), lambda b, t: (b, 0, t, 0, 0)),
        pl.BlockSpec((1, tm, MIX_WIDTH), lambda b, t: (b, t, 0)),
    ]
    shapes = [
        jax.ShapeDtypeStruct((B, S, MIX_WIDTH), jnp.bfloat16),
        jax.ShapeDtypeStruct((B, n_kv, S, HEAD_DIM), jnp.bfloat16),
        jax.ShapeDtypeStruct((B, n_kv, S // v_blk, V_ROWS, v_blk), jnp.bfloat16),
        jax.ShapeDtypeStruct((B, S, MIX_WIDTH), jnp.bfloat16),
    ]
    return specs, shapes


def _resident(shape):
    return pl.BlockSpec(shape, lambda b, t: (0,) * len(shape), pipeline_mode=pl.Buffered(1))


def _in_proj(h, g, w, tabs, n_kv, v_blk):
    B, S, _ = h.shape
    tm = TOKEN_TILE
    tab_spec = pl.BlockSpec((tm, LANES), lambda b, t: (t, 0))
    out_specs, out_shape = _proj_out_specs(B, S, tm, n_kv, v_blk)
    return pl.pallas_call(
        functools.partial(_in_proj_kernel, n_kv, v_blk),
        grid=(B, S // tm),
        in_specs=[
            pl.BlockSpec((1, tm, D_MODEL), lambda b, t: (b, t, 0)),
            _resident((1, D_MODEL)),
            _resident(w.shape),
            tab_spec, tab_spec, tab_spec, tab_spec,
        ],
        out_specs=out_specs,
        out_shape=out_shape,
        compiler_params=pltpu.CompilerParams(
            dimension_semantics=("parallel", "parallel"), vmem_limit_bytes=VMEM_LIMIT),
        name="in_proj",
    )(h, g, w, *tabs)


def _mid(o, sz, h, w_o, g, w, tabs, n_kv, v_blk):
    B, S, _ = h.shape
    tm = TOKEN_TILE
    row_spec = pl.BlockSpec((1, tm, D_MODEL), lambda b, t: (b, t, 0))
    tab_spec = pl.BlockSpec((tm, LANES), lambda b, t: (t, 0))
    out_specs, out_shape = _proj_out_specs(B, S, tm, n_kv, v_blk)
    return pl.pallas_call(
        functools.partial(_mid_kernel, n_kv, v_blk),
        grid=(B, S // tm),
        in_specs=[
            row_spec, row_spec, row_spec,
            _resident(w_o.shape),
            _resident((1, D_MODEL)),
            _resident(w.shape),
            tab_spec, tab_spec, tab_spec, tab_spec,
        ],
        out_specs=[row_spec] + out_specs,
        out_shape=[jax.ShapeDtypeStruct((B, S, D_MODEL), jnp.float32)] + out_shape,
        compiler_params=pltpu.CompilerParams(
            dimension_semantics=("parallel", "parallel"), vmem_limit_bytes=VMEM_LIMIT),
        name="mid",
    )(o, sz, h, w_o, g, w, *tabs)


def _out_proj_kernel(o_ref, z_ref, h_ref, w_ref, g_ref, out_ref):
    h = h_ref[0] + _dot(o_ref[0] * z_ref[0], w_ref[...])
    ms = jnp.mean(h * h, axis=-1, keepdims=True)
    out_ref[0] = h * lax.rsqrt(ms + NORM_EPS) * g_ref[...]


def _out_proj(o, sz, h, w, g):
    B, S, _ = h.shape
    tm = TOKEN_TILE
    row_spec = pl.BlockSpec((1, tm, D_MODEL), lambda b, t: (b, t, 0))
    return pl.pallas_call(
        _out_proj_kernel,
        grid=(B, S // tm),
        in_specs=[row_spec, row_spec, row_spec, _resident(w.shape), _resident((1, D_MODEL))],
        out_specs=row_spec,
        out_shape=jax.ShapeDtypeStruct((B, S, D_MODEL), jnp.float32),
        compiler_params=pltpu.CompilerParams(
            dimension_semantics=("parallel", "parallel"), vmem_limit_bytes=VMEM_LIMIT),
        name="out_proj",
    )(o, sz, h, w, g)


def _stack_heads(q, n):
    return jnp.concatenate([q[:, g * HEAD_DIM:(g + 1) * HEAD_DIM] for g in range(n)], axis=0)


def _unstack_heads_t(ot, n, rows):
    st = jnp.concatenate([ot[:, g * rows:(g + 1) * rows] for g in range(n)], axis=0)
    return st.T


def _swa_kernel(q_ref, k_ref, vt_ref, sink_ref, mask_ref, o_ref, s_ref, p_ref):
    n = pl.program_id(1)
    n_kv = k_ref.shape[1]
    gw = q_ref.shape[2] // n_kv
    group = gw // HEAD_DIM
    n_sub = q_ref.shape[1] // WINDOW
    units = [(c, sb) for sb in range(n_sub) for c in range(n_kv)]
    mx, m = {}, {}

    def band(sb):
        qb = n * n_sub + sb
        return qb, jnp.maximum(qb - 1, 0)

    def scores(u):
        c, sb = units[u]
        qb, first = band(sb)
        qs = _stack_heads(q_ref[0, sb * WINDOW:(sb + 1) * WINDOW, c * gw:(c + 1) * gw], group)
        kb = k_ref[0, c, pl.ds(pl.multiple_of(first * WINDOW, WINDOW), 2 * WINDOW), :]
        st = _dot_nt(kb, qs) + mask_ref[jnp.where(qb == 0, 1, 0)]
        s_ref[u % 3] = st
        mx[u] = jnp.max(st, axis=0, keepdims=True)

    def probs(u):
        c, _ = units[u]
        m[u] = jnp.maximum(mx[u], sink_ref[c])
        p_ref[u % 2] = jnp.exp2(s_ref[u % 3 + jnp.minimum(n, 0)] - m[u]).astype(jnp.bfloat16)

    def output(u):
        c, sb = units[u]
        _, first = band(sb)
        vt = jnp.concatenate([vt_ref[0, c, first], vt_ref[0, c, first + 1]], axis=1)
        acc = _dot(vt, p_ref[u % 2])
        denom = acc[HEAD_DIM:HEAD_DIM + 1] + jnp.exp2(sink_ref[c] - m[u])
        ot = acc[:HEAD_DIM] / denom
        o_ref[0, sb * WINDOW:(sb + 1) * WINDOW, c * gw:(c + 1) * gw] = _unstack_heads_t(
            ot, group, WINDOW).astype(jnp.bfloat16)

    for slot in range(len(units) + 2):
        if slot < len(units):
            scores(slot)
        if 0 <= slot - 1 < len(units):
            probs(slot - 1)
        if 0 <= slot - 2 < len(units):
            output(slot - 2)


def _swa_mask(rows):
    key = lax.broadcasted_iota(jnp.int32, (2 * WINDOW, rows), 0)
    qpos = lax.broadcasted_iota(jnp.int32, (2 * WINDOW, rows), 1) % WINDOW
    masks = []
    for off in (WINDOW, 0):
        dist = qpos + off - key
        masks.append(jnp.where((dist >= 0) & (dist < WINDOW), 0.0, NEG))
    return jnp.stack(masks).astype(jnp.float32)


def _swa(q, k, vt, sinks2):
    B, S, _ = q.shape
    n_kv = k.shape[1]
    rows = (MIX_WIDTH // n_kv // HEAD_DIM) * WINDOW
    tq = SWA_Q_TILE
    return pl.pallas_call(
        _swa_kernel,
        grid=(B, S // tq),
        in_specs=[
            pl.BlockSpec((1, tq, MIX_WIDTH), lambda b, n: (b, n, 0)),
            pl.BlockSpec((1, n_kv, S, HEAD_DIM), lambda b, n: (b, 0, 0, 0)),
            pl.BlockSpec((1, n_kv, S // WINDOW, V_ROWS, WINDOW), lambda b, n: (b, 0, 0, 0, 0)),
            pl.BlockSpec((n_kv, 1, rows), lambda b, n: (0, 0, 0)),
            pl.BlockSpec((2, 2 * WINDOW, rows), lambda b, n: (0, 0, 0)),
        ],
        out_specs=pl.BlockSpec((1, tq, MIX_WIDTH), lambda b, n: (b, n, 0)),
        out_shape=jax.ShapeDtypeStruct((B, S, MIX_WIDTH), jnp.bfloat16),
        scratch_shapes=[
            pltpu.VMEM((3, 2 * WINDOW, rows), jnp.float32),
            pltpu.VMEM((2, 2 * WINDOW, rows), jnp.bfloat16),
        ],
        compiler_params=pltpu.CompilerParams(
            dimension_semantics=("parallel", "arbitrary"),
            vmem_limit_bytes=VMEM_LIMIT),
        name="swa",
    )(q, k, vt, sinks2, _swa_mask(rows))


def _moba_kernel(q_ref, k_ref, vt_ref, cmask_ref, o_ref, km_ref, qs_ref, bias_ref,
                 s0_ref, s1_ref, mx0_ref, mx1_ref, m_ref, acc_ref):
    i = pl.program_id(1)
    n_kv = k_ref.shape[1]
    nb = k_ref.shape[2] // MOBA_BLOCK
    gw = q_ref.shape[2] // n_kv
    group = gw // HEAD_DIM
    rows = group * MOBA_BLOCK

    @pl.when(i == 0)
    def _():
        for c in range(n_kv):
            means = [jnp.sum(k_ref[0, c, j * MOBA_BLOCK:(j + 1) * MOBA_BLOCK, :].astype(jnp.float32),
                             axis=0, keepdims=True) for j in range(nb)]
            km = jnp.concatenate(means, axis=0) * (1.0 / MOBA_BLOCK)
            hi = km.astype(jnp.bfloat16)
            km_ref[c, 0:nb] = hi
            km_ref[c, nb:2 * nb] = (km - hi.astype(jnp.float32)).astype(jnp.bfloat16)

    bufs = ((s0_ref, mx0_ref), (s1_ref, mx1_ref))

    def keys(c, j):
        return k_ref[0, c, pl.ds(pl.multiple_of(j * MOBA_BLOCK, MOBA_BLOCK), MOBA_BLOCK), :]

    def scores(c, t):
        kj = k_ref[0, c, pl.ds(pl.multiple_of(t * MOBA_BLOCK, MOBA_BLOCK), 2 * MOBA_BLOCK), :]
        for g in range(group):
            cols = slice(g * MOBA_BLOCK, (g + 1) * MOBA_BLOCK)
            s = _dot_nt(kj, qs_ref[c, cols, :])
            for par, (s_ref, mx_ref) in enumerate(bufs):
                blk = s[par * MOBA_BLOCK:(par + 1) * MOBA_BLOCK]
                s_ref[c, :, cols] = blk
                mx_ref[c, :, cols] = jnp.max(blk, axis=0, keepdims=True)

    def score_tile(c, j, g, s_ref, mx_ref):
        cols = slice(g * MOBA_BLOCK, (g + 1) * MOBA_BLOCK)
        s = _dot_nt(keys(c, j), qs_ref[c, cols, :])
        s_ref[c, :, cols] = s
        mx_ref[c, :, cols] = jnp.max(s, axis=0, keepdims=True)

    def accumulate_tile(c, j, cols, p, alpha):
        acc_ref[c, :, cols] = acc_ref[c, :, cols] * alpha + _dot(vt_ref[0, c, j], p)

    def select_blocks(c):
        gt = _dot_nt(km_ref[c], qs_ref[c])
        blk = lax.broadcasted_iota(jnp.int32, (nb, rows), 0).astype(jnp.float32)
        past = blk < i.astype(jnp.float32)
        gate = jnp.where(past, gt[0:nb] + gt[nb:2 * nb], NEG)
        chosen = jnp.zeros((nb, rows), jnp.bool_)
        for _ in range(MOBA_TOPK):
            cur = jnp.where(chosen, BELOW_NEG, gate)
            best = jnp.max(cur, axis=0, keepdims=True)
            first = jnp.min(jnp.where(cur == best, blk, float(nb)), axis=0, keepdims=True)
            chosen = chosen | (blk == first)
        bias_ref[c] = jnp.where(chosen & past, 0.0, NEG)

    ds, dm, dp = {}, {}, {}

    def own_scores(c):
        ds[c] = _dot_nt(keys(c, i), qs_ref[c]) + cmask_ref[...]
        dm[c] = jnp.max(ds[c], axis=0, keepdims=True)

    def own_probs(c):
        m_ref[c] = dm[c]
        dp[c] = jnp.exp2(ds[c] - dm[c]).astype(jnp.bfloat16)

    def own_out(c):
        acc_ref[c] = _dot(vt_ref[0, c, i], dp[c])

    for c in range(n_kv):
        qs_ref[c] = _stack_heads(q_ref[0, :, c * gw:(c + 1) * gw], group)
    for slot in range(n_kv + 2):
        if slot < n_kv:
            own_scores(slot)
            select_blocks(slot)
            if slot < n_kv - 1:
                scores(slot, 0)
        if 0 <= slot - 1 < n_kv:
            own_probs(slot - 1)
        if 0 <= slot - 2 < n_kv:
            own_out(slot - 2)

    def pair(t, refill):
        pending = None
        for c in range(n_kv):
            ahead = (n_kv - 1, t) if c == 0 else ((c - 1, t + 2) if refill else None)
            m = m_ref[c]
            stats = []
            for par, (s_ref, mx_ref) in enumerate(bufs):
                bias = bias_ref[c, pl.ds(t + par, 1), :]
                m_new = jnp.maximum(m, mx_ref[c] + bias)
                stats.append((m_new - bias, jnp.exp2(m - m_new)))
                m = m_new
            m_ref[c] = m
            for g in range(group):
                cols = slice(g * MOBA_BLOCK, (g + 1) * MOBA_BLOCK)
                for par, (s_ref, mx_ref) in enumerate(bufs):
                    if ahead is not None:
                        score_tile(ahead[0], ahead[1] + par, g, s_ref, mx_ref)
                    shift, alpha = stats[par]
                    p = jnp.exp2(s_ref[c, :, cols] - shift[:, cols]).astype(jnp.bfloat16)
                    if pending is not None:
                        accumulate_tile(*pending)
                    pending = (c, t + par, cols, p, alpha[:, cols])
        accumulate_tile(*pending)

    n_pairs = (i + 1) // 2

    def steady(u, carry):
        pair(2 * u, True)
        return carry

    lax.fori_loop(0, n_pairs - 1, steady, 0)

    @pl.when(n_pairs > 0)
    def _():
        pair(2 * (n_pairs - 1), False)

    for c in range(n_kv):
        acc = acc_ref[c]
        ot = acc[:HEAD_DIM] / acc[HEAD_DIM:HEAD_DIM + 1]
        o_ref[0, :, c * gw:(c + 1) * gw] = _unstack_heads_t(ot, group, MOBA_BLOCK).astype(jnp.bfloat16)


def _moba_causal_mask(rows):
    key = lax.broadcasted_iota(jnp.int32, (MOBA_BLOCK, rows), 0)
    qpos = lax.broadcasted_iota(jnp.int32, (MOBA_BLOCK, rows), 1) % MOBA_BLOCK
    return jnp.where(key <= qpos, 0.0, NEG).astype(jnp.float32)


def _moba(q, k, vt):
    B, S, _ = q.shape
    n_kv = k.shape[1]
    nb = S // MOBA_BLOCK
    rows = (MIX_WIDTH // n_kv // HEAD_DIM) * MOBA_BLOCK
    return pl.pallas_call(
        _moba_kernel,
        grid=(B, nb),
        in_specs=[
            pl.BlockSpec((1, MOBA_BLOCK, MIX_WIDTH), lambda b, i: (b, i, 0)),
            pl.BlockSpec((1, n_kv, S, HEAD_DIM), lambda b, i: (b, 0, 0, 0)),
            pl.BlockSpec((1, n_kv, nb, V_ROWS, MOBA_BLOCK), lambda b, i: (b, 0, 0, 0, 0)),
            pl.BlockSpec((MOBA_BLOCK, rows), lambda b, i: (0, 0)),
        ],
        out_specs=pl.BlockSpec((1, MOBA_BLOCK, MIX_WIDTH), lambda b, i: (b, i, 0)),
        out_shape=jax.ShapeDtypeStruct((B, S, MIX_WIDTH), jnp.bfloat16),
        scratch_shapes=[
            pltpu.VMEM((n_kv, 2 * nb, HEAD_DIM), jnp.bfloat16),
            pltpu.VMEM((n_kv, rows, HEAD_DIM), jnp.bfloat16),
            pltpu.VMEM((n_kv, nb, rows), jnp.float32),
            pltpu.VMEM((n_kv, MOBA_BLOCK, rows), jnp.float32),
            pltpu.VMEM((n_kv, MOBA_BLOCK, rows), jnp.float32),
            pltpu.VMEM((n_kv, 1, rows), jnp.float32),
            pltpu.VMEM((n_kv, 1, rows), jnp.float32),
            pltpu.VMEM((n_kv, 1, rows), jnp.float32),
            pltpu.VMEM((n_kv, V_ROWS, rows), jnp.float32),
        ],
        compiler_params=pltpu.CompilerParams(
            dimension_semantics=("parallel", "arbitrary"),
            vmem_limit_bytes=VMEM_LIMIT),
        name="moba",
    )(q, k, vt, _moba_causal_mask(rows))


def _rope_tables(seq, q_scale):
    pos = jnp.arange(seq, dtype=jnp.float32)
    inv = ROPE_THETA ** (-jnp.arange(0, HEAD_DIM, 2, dtype=jnp.float32) / HEAD_DIM)
    ang = pos[:, None] * inv[None, :]
    cos, sin = jnp.cos(ang), jnp.sin(ang)
    ck = jnp.tile(cos, (1, LANES // (HEAD_DIM // 2)))
    sk = jnp.tile(jnp.concatenate([-sin, sin], axis=1), (1, LANES // HEAD_DIM))
    return ck * q_scale, sk * q_scale, ck, sk


def kernel(x, norm_g, w_in_a, sinks_a, w_in_b, w_out, final_g):
    B, S, _ = x.shape
    depth = norm_g.shape[0]
    assert S % TOKEN_TILE == 0 and S % SWA_Q_TILE == 0 and S >= 2 * MOBA_BLOCK
    tabs = _rope_tables(S, LOG2E / math.sqrt(HEAD_DIM))

    def layer_params(i):
        g = norm_g[i].reshape(1, D_MODEL)
        if i % 2 == 0:
            return g, w_in_a[i // 2].astype(jnp.bfloat16), N_KV_A, WINDOW
        return g, w_in_b[i // 2].astype(jnp.bfloat16), N_KV_B, MOBA_BLOCK

    h = x
    q, k, vt, sz = _in_proj(h, *layer_params(0)[:2], tabs, *layer_params(0)[2:])
    for i in range(depth):
        if i % 2 == 0:
            group = N_HEADS // N_KV_A
            sinks2 = jnp.repeat(sinks_a[i // 2].astype(jnp.float32) * LOG2E, WINDOW).reshape(
                N_KV_A, 1, group * WINDOW)
            o = _swa(q, k, vt, sinks2)
        else:
            o = _moba(q, k, vt)
        w_o = w_out[i].astype(jnp.bfloat16)
        if i + 1 < depth:
            g, w, n_kv, v_blk = layer_params(i + 1)
            h, q, k, vt, sz = _mid(o, sz, h, w_o, g, w, tabs, n_kv, v_blk)
        else:
            h = _out_proj(o, sz, h, w_o, final_g.reshape(1, D_MODEL))
    return h
```
